```python
import math
import jax, jax.numpy as jnp
from jax import lax
import numpy as np

D_MODEL = 1024
BATCH = 8
SEQ = 2048
DEPTH = 4

GRID_W = 64
CTX_LEN = 256
D_SSM = 512
SSM_GROUP = 16
N_SSM_GROUPS = D_SSM // SSM_GROUP
SSM_STATE = 64
D_CONV = 512
CONV_WIDTH = 31
CONV_PAD = CONV_WIDTH // 2
D_MIX = D_SSM + D_CONV
D_IN = D_SSM + 2 * D_CONV
D_FF = 2816
N_EXPERTS = 8
TOP_K = 2
D_FF_EXPERT = 2816
N_MOD = 6
EPS = 1e-6
DT_MIN = 1e-3
DT_MAX = 1e-1

kernel_name = "hybrid_s5_conformer_moe_dit"


def rmsnorm(x, g):
    x32 = x.astype(jnp.float32)
    y = x32 * lax.rsqrt(jnp.mean(x32 * x32, axis=-1, keepdims=True) + EPS)
    return (y * g.astype(jnp.float32)).astype(x.dtype)


def layernorm(x, g, b):
    x32 = x.astype(jnp.float32)
    mu = jnp.mean(x32, axis=-1, keepdims=True)
    xc = x32 - mu
    y = xc * lax.rsqrt(jnp.mean(xc * xc, axis=-1, keepdims=True) + EPS)
    return (y * g.astype(jnp.float32) + b.astype(jnp.float32)).astype(x.dtype)


def modulate(h, shift, scale):
    return h * (1 + scale) + shift


def s5_discretize(lam_re, lam_im, log_dt, b_re, b_im):
    lam = lax.complex(lam_re.astype(jnp.float32), lam_im.astype(jnp.float32))
    dt = jnp.exp(log_dt.astype(jnp.float32))[:, None]
    lam_bar = jnp.exp(lam * dt)
    b = lax.complex(b_re.astype(jnp.float32), b_im.astype(jnp.float32))
    b_bar = ((lam_bar - 1.0) / lam)[..., None] * b
    return lam_bar, b_bar


def diag_scan(lam_bar, bu, h0, reverse):
    if h0 is not None:
        idx = bu.shape[1] - 1 if reverse else 0
        bu = bu.at[:, idx].add(lam_bar * h0)
    a = jnp.broadcast_to(lam_bar, bu.shape)

    def combine(left, right):
        a_l, b_l = left
        a_r, b_r = right
        return a_l * a_r, a_r * b_l + b_r

    _, h = lax.associative_scan(combine, (a, bu), reverse=reverse, axis=1)
    return h


def s5_glu(y, d_skip, u, w_glu, b_glu, dtype):
    y = y + d_skip.astype(jnp.float32) * u.astype(jnp.float32)
    z = jax.nn.gelu(y)
    out = z * jax.nn.sigmoid(z @ w_glu.astype(jnp.float32) + b_glu.astype(jnp.float32))
    return out.astype(dtype)


def s5_bidirectional(u_lat, u_ctx, lam_re, lam_im, log_dt, b_re, b_im, c_re, c_im,
                     d_skip, w_glu, b_glu, need_ctx):
    dtype = u_lat.dtype

    def to_groups(u):
        return u.astype(jnp.float32).reshape(u.shape[0], u.shape[1], N_SSM_GROUPS, SSM_GROUP).astype(jnp.complex64)

    ug_lat, ug_ctx = to_groups(u_lat), to_groups(u_ctx)
    y_lat = jnp.zeros(u_lat.shape, jnp.float32)
    y_ctx = jnp.zeros(u_ctx.shape, jnp.float32)
    for direction in range(2):
        reverse = direction == 1
        lam_bar, b_bar = s5_discretize(lam_re[direction], lam_im[direction], log_dt[direction],
                                       b_re[direction], b_im[direction])
        cmat = lax.complex(c_re[direction].astype(jnp.float32), c_im[direction].astype(jnp.float32))
        s_ctx = diag_scan(lam_bar, jnp.einsum('blgh,gph->blgp', ug_ctx, b_bar), None, reverse)
        s_end = s_ctx[:, 0] if reverse else s_ctx[:, -1]
        s_lat = diag_scan(lam_bar, jnp.einsum('blgh,gph->blgp', ug_lat, b_bar), s_end, reverse)
        y_lat = y_lat + jnp.einsum('blgp,ghp->blgh', s_lat, cmat).real.reshape(u_lat.shape)
        if need_ctx:
            y_ctx = y_ctx + jnp.einsum('blgp,ghp->blgh', s_ctx, cmat).real.reshape(u_ctx.shape)
    out_lat = s5_glu(y_lat, d_skip, u_lat, w_glu, b_glu, dtype)
    out_ctx = s5_glu(y_ctx, d_skip, u_ctx, w_glu, b_glu, dtype) if need_ctx else None
    return out_lat, out_ctx


def conformer_conv(v, g, w_dw, b_dw, ln_g, ln_b):
    a = v * jax.nn.sigmoid(g)
    y = lax.conv_general_dilated(a, w_dw[:, None, :].astype(a.dtype), window_strides=(1,),
                                 padding=[(CONV_PAD, CONV_PAD)],
                                 dimension_numbers=('NWC', 'WIO', 'NWC'),
                                 feature_group_count=D_CONV) + b_dw
    return jax.nn.silu(layernorm(y, ln_g, ln_b))


def hybrid_mixer(a_lat, a_ctx, w_in, lam_re, lam_im, log_dt, b_re, b_im, c_re, c_im, d_skip,
                 w_glu, b_glu, conv_w, conv_b, ln_g, ln_b, w_out, need_ctx):
    p_lat = a_lat @ w_in
    p_ctx = a_ctx @ w_in
    s_lat, s_ctx = s5_bidirectional(p_lat[..., :D_SSM], p_ctx[..., :D_SSM], lam_re, lam_im, log_dt,
                                    b_re, b_im, c_re, c_im, d_skip, w_glu, b_glu, need_ctx)
    cv_lat = conformer_conv(p_lat[..., D_SSM:D_SSM + D_CONV], p_lat[..., D_SSM + D_CONV:],
                            conv_w, conv_b, ln_g, ln_b)
    y_lat = jnp.concatenate([s_lat, cv_lat], axis=-1) @ w_out
    if not need_ctx:
        return y_lat, None
    cv_ctx = conformer_conv(p_ctx[..., D_SSM:D_SSM + D_CONV], p_ctx[..., D_SSM + D_CONV:],
                            conv_w, conv_b, ln_g, ln_b)
    y_ctx = jnp.concatenate([s_ctx, cv_ctx], axis=-1) @ w_out
    return y_lat, y_ctx


def swiglu(h, w_gate, w_up, w_down):
    return (jax.nn.silu(h @ w_gate) * (h @ w_up)) @ w_down


def moe_swiglu(h, w_router, b_router, w_gate, w_up, w_down):
    logits = (h @ w_router).astype(jnp.float32) + b_router.astype(jnp.float32)
    top_v, top_i = lax.top_k(logits, TOP_K)
    top_w = jax.nn.softmax(top_v, axis=-1)
    gates = jnp.sum(jax.nn.one_hot(top_i, N_EXPERTS, dtype=jnp.float32) * top_w[..., None], axis=-2)
    gates = gates.astype(h.dtype)
    out = jnp.zeros(h.shape, h.dtype)
    for e in range(N_EXPERTS):
        out = out + gates[..., e:e + 1] * swiglu(h, w_gate[e], w_up[e], w_down[e])
    return out


def setup_inputs(seed: int = 0) -> dict:
    key = jax.random.key(seed)
    ks = iter(jax.random.split(key, 48))
    f32 = jnp.float32

    def nrm(shape, s):
        return s * jax.random.normal(next(ks), shape, f32)

    n_dense = (DEPTH + 1) // 2
    n_moe = DEPTH // 2
    G, P, H = N_SSM_GROUPS, SSM_STATE, SSM_GROUP
    return {
        "x": nrm((BATCH, SEQ, D_MODEL), 1.0),
        "c": nrm((BATCH, D_MODEL), 1.0),
        "ctx": nrm((BATCH, CTX_LEN, D_MODEL), 1.0),
        "c_ctx": nrm((D_MODEL,), 1.0),
        "w_ada": nrm((DEPTH, D_MODEL, N_MOD * D_MODEL), 0.5 * D_MODEL ** -0.5),
        "b_ada": nrm((DEPTH, N_MOD * D_MODEL), 0.02),
        "g_pre_mix": 1.0 + nrm((DEPTH, D_MODEL), 0.05),
        "g_post_mix": 1.0 + nrm((DEPTH, D_MODEL), 0.05),
        "g_pre_ffn": 1.0 + nrm((DEPTH, D_MODEL), 0.05),
        "g_post_ffn": 1.0 + nrm((DEPTH, D_MODEL), 0.05),
        "w_in": nrm((DEPTH, D_MODEL, D_IN), D_MODEL ** -0.5),
        "ssm_lam_re": -0.5 + nrm((DEPTH, 2, G, P), 0.01),
        "ssm_lam_im": jnp.pi * jnp.arange(P, dtype=f32) + nrm((DEPTH, 2, G, P), 0.01),
        "ssm_log_dt": jax.random.uniform(next(ks), (DEPTH, 2, G), f32, math.log(DT_MIN), math.log(DT_MAX)),
        "ssm_b_re": nrm((DEPTH, 2, G, P, H), (2 * H) ** -0.5),
        "ssm_b_im": nrm((DEPTH, 2, G, P, H), (2 * H) ** -0.5),
        "ssm_c_re": nrm((DEPTH, 2, G, H, P), (2 * P) ** -0.5),
        "ssm_c_im": nrm((DEPTH, 2, G, H, P), (2 * P) ** -0.5),
        "ssm_d": nrm((DEPTH, D_SSM), 1.0),
        "ssm_w_glu": nrm((DEPTH, D_SSM, D_SSM), D_SSM ** -0.5),
        "ssm_b_glu": nrm((DEPTH, D_SSM), 0.02),
        "conv_w": nrm((DEPTH, CONV_WIDTH, D_CONV), CONV_WIDTH ** -0.5),
        "conv_b": nrm((DEPTH, D_CONV), 0.02),
        "conv_ln_g": 1.0 + nrm((DEPTH, D_CONV), 0.05),
        "conv_ln_b": nrm((DEPTH, D_CONV), 0.02),
        "w_out": nrm((DEPTH, D_MIX, D_MODEL), D_MIX ** -0.5),
        "ffn_w_gate": nrm((n_dense, D_MODEL, D_FF), D_MODEL ** -0.5),
        "ffn_w_up": nrm((n_dense, D_MODEL, D_FF), D_MODEL ** -0.5),
        "ffn_w_down": nrm((n_dense, D_FF, D_MODEL), D_FF ** -0.5),
        "moe_w_router": nrm((n_moe, D_MODEL, N_EXPERTS), D_MODEL ** -0.5),
        "moe_b_router": nrm((n_moe, N_EXPERTS), 0.01),
        "moe_w_gate": nrm((n_moe, N_EXPERTS, D_MODEL, D_FF_EXPERT), D_MODEL ** -0.5),
        "moe_w_up": nrm((n_moe, N_EXPERTS, D_MODEL, D_FF_EXPERT), D_MODEL ** -0.5),
        "moe_w_down": nrm((n_moe, N_EXPERTS, D_FF_EXPERT, D_MODEL), D_FF_EXPERT ** -0.5),
    }


def reference(x, c, ctx, c_ctx, w_ada, b_ada, g_pre_mix, g_post_mix, g_pre_ffn, g_post_ffn, w_in,
              ssm_lam_re, ssm_lam_im, ssm_log_dt, ssm_b_re, ssm_b_im, ssm_c_re, ssm_c_im, ssm_d,
              ssm_w_glu, ssm_b_glu, conv_w, conv_b, conv_ln_g, conv_ln_b, w_out,
              ffn_w_gate, ffn_w_up, ffn_w_down, moe_w_router, moe_b_router, moe_w_gate, moe_w_up, moe_w_down):
    silu_c = jax.nn.silu(c)[:, None, :]
    silu_cc = jax.nn.silu(c_ctx)[None, None, :]
    h_lat, h_ctx = x, ctx
    for l in range(DEPTH):
        need_ctx = l < DEPTH - 1
        mod_lat = jnp.split(silu_c @ w_ada[l] + b_ada[l], N_MOD, axis=-1)
        mod_ctx = jnp.split(silu_cc @ w_ada[l] + b_ada[l], N_MOD, axis=-1)

        a_lat = modulate(rmsnorm(h_lat, g_pre_mix[l]), mod_lat[0], mod_lat[1])
        a_ctx = modulate(rmsnorm(h_ctx, g_pre_mix[l]), mod_ctx[0], mod_ctx[1])
        y_lat, y_ctx = hybrid_mixer(a_lat, a_ctx, w_in[l], ssm_lam_re[l], ssm_lam_im[l], ssm_log_dt[l],
                                    ssm_b_re[l], ssm_b_im[l], ssm_c_re[l], ssm_c_im[l], ssm_d[l],
                                    ssm_w_glu[l], ssm_b_glu[l], conv_w[l], conv_b[l], conv_ln_g[l],
                                    conv_ln_b[l], w_out[l], need_ctx)
        h_lat = h_lat + mod_lat[2] * rmsnorm(y_lat, g_post_mix[l])
        if need_ctx:
            h_ctx = h_ctx + mod_ctx[2] * rmsnorm(y_ctx, g_post_mix[l])

        f_lat = modulate(rmsnorm(h_lat, g_pre_ffn[l]), mod_lat[3], mod_lat[4])
        if l % 2 == 0:
            i = l // 2
            ffn = lambda h, i=i: swiglu(h, ffn_w_gate[i], ffn_w_up[i], ffn_w_down[i])
        else:
            i = l // 2
            ffn = lambda h, i=i: moe_swiglu(h, moe_w_router[i], moe_b_router[i], moe_w_gate[i],
                                            moe_w_up[i], moe_w_down[i])
        h_lat = h_lat + mod_lat[5] * rmsnorm(ffn(f_lat), g_post_ffn[l])
        if need_ctx:
            f_ctx = modulate(rmsnorm(h_ctx, g_pre_ffn[l]), mod_ctx[3], mod_ctx[4])
            h_ctx = h_ctx + mod_ctx[5] * rmsnorm(ffn(f_ctx), g_post_ffn[l])
    return h_lat
```

```python
import functools
import math

import jax
import jax.numpy as jnp
from jax import lax
from jax.experimental import pallas as pl
from jax.experimental.pallas import tpu as pltpu

F32 = jnp.float32
BF16 = jnp.bfloat16

D_MODEL = 1024
BATCH = 8
SEQ = 2048
CTX_LEN = 256
DEPTH = 4
D_SSM = 512
SSM_GROUP = 16
N_SSM_GROUPS = 32
SSM_STATE = 64
D_CONV = 512
CONV_WIDTH = 31
CONV_PAD = CONV_WIDTH // 2
D_IN = D_SSM + 2 * D_CONV
D_FF = 2816
N_EXPERTS = 8
N_MOD = 6
EPS = 1e-6

N_CTX = CTX_LEN * BATCH
N_LAT = SEQ * BATCH
N_ROWS = N_CTX + N_LAT
N_STATE = N_SSM_GROUPS * SSM_STATE

SUBLANES = 8
LANES = 128
VMEM_LIMIT = 56 * 1024 * 1024

TM = 512
SCAN_T = 64
SCAN_R = SCAN_T * BATCH
SCAN_SLABS = 2
SLAB_CH = D_SSM // SCAN_SLABS
SLAB_ST = N_STATE // SCAN_SLABS
SCAN_W = 512
HALO = 128
CONV_SUB = 32
FF_CHUNK = 1408
MOE_TM = 1024


def _dot(a, b):
    return jnp.dot(a, b, preferred_element_type=F32)


def _rms(x, g):
    ms = jnp.mean(x * x, axis=-1, keepdims=True)
    return x * lax.rsqrt(ms + EPS) * g


def _tile_mul_add(y, scale, shift):
    rows, d = y.shape
    y3 = y.reshape(rows // SUBLANES, SUBLANES, d)
    return (y3 * (1.0 + scale)[None] + shift[None]).reshape(rows, d)


def _tile_gate_add(h, gate, r):
    rows, d = h.shape
    r3 = r.reshape(rows // SUBLANES, SUBLANES, d)
    return h + (gate[None] * r3).reshape(rows, d)


def _params(sem):
    return pltpu.CompilerParams(dimension_semantics=sem, vmem_limit_bytes=VMEM_LIMIT)


def _const_spec(shape):
    nd = len(shape)
    return pl.BlockSpec(shape, lambda *_: (0,) * nd)


def _ada_kernel(c_ref, w_ref, b_ref, o_ref):
    c = c_ref[...]
    s = c * jax.nn.sigmoid(c)
    o_ref[0] = _dot(s, w_ref[0]) + b_ref[0]


def _ada_call(cin, w_ada, b_ada):
    tn = 1536
    return pl.pallas_call(
        _ada_kernel,
        grid=(DEPTH, N_MOD * D_MODEL // tn),
        in_specs=[
            pl.BlockSpec((2 * BATCH, D_MODEL), lambda l, j: (0, 0)),
            pl.BlockSpec((1, D_MODEL, tn), lambda l, j: (l, 0, j)),
            pl.BlockSpec((1, 1, tn), lambda l, j: (l, 0, j)),
        ],
        out_specs=pl.BlockSpec((1, 2 * BATCH, tn), lambda l, j: (l, 0, j)),
        out_shape=jax.ShapeDtypeStruct((DEPTH, 2 * BATCH, N_MOD * D_MODEL), F32),
        compiler_params=_params(("arbitrary", "arbitrary")),
        name="ada_mod",
    )(cin, w_ada, b_ada.reshape(DEPTH, 1, N_MOD * D_MODEL))


def _mix_in_kernel(h_ref, mod_ref, g_ref, w_ref, u_ref, a_ref):
    y = _rms(h_ref[...], g_ref[...])
    a = _tile_mul_add(y, mod_ref[0, 1], mod_ref[0, 0]).astype(BF16)
    p = _dot(a, w_ref[...])
    u_ref[...] = p[:, :D_SSM]
    v = p[:, D_SSM:D_SSM + D_CONV]
    g = p[:, D_SSM + D_CONV:]
    a_ref[...] = v * jax.nn.sigmoid(g)


def _seg_index(i, off):
    return ((i + off) >= (N_CTX // TM)).astype(jnp.int32)


def _mix_in_call(hs, mod, g_pre, w_in):
    nt = N_ROWS // TM
    return pl.pallas_call(
        _mix_in_kernel,
        grid=(nt,),
        in_specs=[
            pl.BlockSpec((TM, D_MODEL), lambda i: (i, 0)),
            pl.BlockSpec((1, N_MOD, SUBLANES, D_MODEL), lambda i: (_seg_index(i, 0), 0, 0, 0)),
            _const_spec((1, D_MODEL)),
            _const_spec((D_MODEL, D_IN)),
        ],
        out_specs=[
            pl.BlockSpec((TM, D_SSM), lambda i: (i, 0)),
            pl.BlockSpec((TM, D_CONV), lambda i: (i, 0)),
        ],
        out_shape=[
            jax.ShapeDtypeStruct((N_ROWS, D_SSM), F32),
            jax.ShapeDtypeStruct((N_ROWS, D_CONV), F32),
        ],
        compiler_params=_params(("arbitrary",)),
        name="mix_in",
    )(hs, mod, g_pre, w_in)


def _scan_kernel(uf_ref, ub_ref, bre_ref, bim_ref, cre_ref, cim_ref, lr_ref, li_ref,
                 yf_ref, yb_ref, hre, him, st_re, st_im):
    @pl.when(pl.program_id(0) == 0)
    def _():
        st_re[...] = jnp.zeros_like(st_re)
        st_im[...] = jnp.zeros_like(st_im)

    for d, u_ref in enumerate((uf_ref, ub_ref)):
        u = u_ref[...].astype(BF16)
        for s in range(SCAN_SLABS):
            us = u[:, s * SLAB_CH:(s + 1) * SLAB_CH]
            hre[d, :, s * SLAB_ST:(s + 1) * SLAB_ST] = _dot(us, bre_ref[d, s])
            him[d, :, s * SLAB_ST:(s + 1) * SLAB_ST] = _dot(us, bim_ref[d, s])

    for d in range(2):
        for c in range(N_STATE // SCAN_W):
            cols = slice(c * SCAN_W, (c + 1) * SCAN_W)
            lam_r = lr_ref[d, :, cols]
            lam_i = li_ref[d, :, cols]

            def step(t, carry, d=d, cols=cols, lam_r=lam_r, lam_i=lam_i):
                sr, si = carry
                tt = t if d == 0 else SCAN_T - 1 - t
                rows = pl.ds(pl.multiple_of(tt * SUBLANES, SUBLANES), SUBLANES)
                nr = lam_r * sr - lam_i * si + hre[d, rows, cols]
                ni = lam_r * si + lam_i * sr + him[d, rows, cols]
                hre[d, rows, cols] = nr
                him[d, rows, cols] = ni
                return nr, ni

            sr, si = lax.fori_loop(0, SCAN_T, step, (st_re[d, :, cols], st_im[d, :, cols]), unroll=8)
            st_re[d, :, cols] = sr
            st_im[d, :, cols] = si

    for d, y_ref in enumerate((yf_ref, yb_ref)):
        for s in range(SCAN_SLABS):
            st = slice(s * SLAB_ST, (s + 1) * SLAB_ST)
            y_ref[:, s * SLAB_CH:(s + 1) * SLAB_CH] = (
                _dot(hre[d, :, st].astype(BF16), cre_ref[d, s])
                + _dot(him[d, :, st].astype(BF16), cim_ref[d, s]))


def _bwd_chunk(k):
    nc = N_CTX // SCAN_R
    n = N_ROWS // SCAN_R
    return jnp.where(k < nc, nc - 1 - k, n - 1 + nc - k)


def _scan_call(u, bre, bim, cre, cim, lam_r, lam_i):
    n = N_ROWS // SCAN_R
    return pl.pallas_call(
        _scan_kernel,
        grid=(n,),
        in_specs=[
            pl.BlockSpec((SCAN_R, D_SSM), lambda k: (k, 0)),
            pl.BlockSpec((SCAN_R, D_SSM), lambda k: (_bwd_chunk(k), 0)),
            _const_spec((2, SCAN_SLABS, SLAB_CH, SLAB_ST)),
            _const_spec((2, SCAN_SLABS, SLAB_CH, SLAB_ST)),
            _const_spec((2, SCAN_SLABS, SLAB_ST, SLAB_CH)),
            _const_spec((2, SCAN_SLABS, SLAB_ST, SLAB_CH)),
            _const_spec((2, SUBLANES, N_STATE)),
            _const_spec((2, SUBLANES, N_STATE)),
        ],
        out_specs=[
            pl.BlockSpec((SCAN_R, D_SSM), lambda k: (k, 0)),
            pl.BlockSpec((SCAN_R, D_SSM), lambda k: (_bwd_chunk(k), 0)),
        ],
        out_shape=[
            jax.ShapeDtypeStruct((N_ROWS, D_SSM), F32),
            jax.ShapeDtypeStruct((N_ROWS, D_SSM), F32),
        ],
        scratch_shapes=[
            pltpu.VMEM((2, SCAN_R, N_STATE), F32),
            pltpu.VMEM((2, SCAN_R, N_STATE), F32),
            pltpu.VMEM((2, SUBLANES, N_STATE), F32),
            pltpu.VMEM((2, SUBLANES, N_STATE), F32),
        ],
        compiler_params=_params(("arbitrary",)),
        name="s5_scan",
    )(u, u, bre, bim, cre, cim, lam_r, lam_i)


def _s5_params(lam_re, lam_im, log_dt, b_re, b_im, c_re, c_im):
    lam = lax.complex(lam_re.astype(F32), lam_im.astype(F32))
    dt = jnp.exp(log_dt.astype(F32))[..., None]
    lam_bar = jnp.exp(lam * dt)
    b = lax.complex(b_re.astype(F32), b_im.astype(F32))
    b_bar = ((lam_bar - 1.0) / lam)[..., None] * b
    gs = N_SSM_GROUPS // SCAN_SLABS
    eye = jnp.eye(gs, dtype=F32)

    def b_blocks(m):
        m = m.reshape(2, SCAN_SLABS, gs, SSM_STATE, SSM_GROUP)
        out = jnp.einsum('dsgph,gk->dsghkp', m, eye)
        return out.reshape(2, SCAN_SLABS, SLAB_CH, SLAB_ST).astype(BF16)

    def c_blocks(m):
        m = m.reshape(2, SCAN_SLABS, gs, SSM_GROUP, SSM_STATE)
        out = jnp.einsum('dsghp,gk->dsgpkh', m, eye)
        return out.reshape(2, SCAN_SLABS, SLAB_ST, SLAB_CH).astype(BF16)

    def lanes(v):
        return jnp.broadcast_to(v.reshape(2, 1, N_STATE), (2, SUBLANES, N_STATE))

    return (b_blocks(jnp.real(b_bar)), b_blocks(jnp.imag(b_bar)),
            c_blocks(c_re.astype(F32)), c_blocks(-c_im.astype(F32)),
            lanes(jnp.real(lam_bar)), lanes(jnp.imag(lam_bar)))


def _mix_out_kernel(with_router, h_ref, u_ref, yf_ref, yb_ref, a_ref, ap_ref, an_ref, mod_ref,
                    dsk_ref, wglu_ref, bglu_ref, cw_ref, cb_ref, lng_ref, lnb_ref, wout_ref,
                    gpost_ref, gffn_ref, *rest):
    if with_router:
        wr_ref, br_ref, hn_ref, f_ref, gates_ref, pad, cvs = rest
    else:
        hn_ref, f_ref, pad, cvs = rest
    i = pl.program_id(0)
    nct = N_CTX // TM
    nt = N_ROWS // TM
    first = jnp.logical_or(i == 0, i == nct)
    last = jnp.logical_or(i == nct - 1, i == nt - 1)
    edge = CONV_PAD * BATCH

    y = yf_ref[...] + yb_ref[...] + dsk_ref[...] * u_ref[...]
    z = jax.nn.gelu(y)
    s = z * jax.nn.sigmoid(_dot(z.astype(BF16), wglu_ref[...]) + bglu_ref[...])

    pad[0:edge, :] = jnp.where(first, 0.0, ap_ref[HALO - edge:HALO, :])
    pad[edge:edge + TM, :] = a_ref[...]
    pad[edge + TM:2 * edge + TM, :] = jnp.where(last, 0.0, an_ref[0:edge, :])

    def conv_block(j, carry):
        r0 = pl.multiple_of(j * CONV_SUB, CONV_SUB)
        acc = jnp.zeros((CONV_SUB // SUBLANES, SUBLANES, D_CONV), F32)
        for k in range(CONV_WIDTH):
            win = pad[pl.ds(r0 + k * BATCH, CONV_SUB), :]
            acc = acc + win.reshape(CONV_SUB // SUBLANES, SUBLANES, D_CONV) * cw_ref[k][None]
        cvs[pl.ds(r0, CONV_SUB), :] = acc.reshape(CONV_SUB, D_CONV)
        return carry

    lax.fori_loop(0, TM // CONV_SUB, conv_block, 0)
    cv = cvs[...] + cb_ref[...]
    mu = jnp.mean(cv, axis=-1, keepdims=True)
    xc = cv - mu
    var = jnp.mean(xc * xc, axis=-1, keepdims=True)
    cv = xc * lax.rsqrt(var + EPS) * lng_ref[...] + lnb_ref[...]
    cv = cv * jax.nn.sigmoid(cv)

    out = _dot(s.astype(BF16), wout_ref[0:D_SSM, :]) + _dot(cv.astype(BF16), wout_ref[D_SSM:, :])
    hn = _tile_gate_add(h_ref[...], mod_ref[0, 2], _rms(out, gpost_ref[...]))
    hn_ref[...] = hn
    f = _tile_mul_add(_rms(hn, gffn_ref[...]), mod_ref[0, 4], mod_ref[0, 3])
    f_ref[...] = f.astype(BF16)

    if with_router:
        logits = jnp.dot(f, wr_ref[...], preferred_element_type=F32,
                         precision=lax.Precision.HIGHEST) + br_ref[...]
        lane = lax.broadcasted_iota(jnp.int32, logits.shape, 1)
        neg = jnp.float32(-jnp.inf)
        logits = jnp.where(lane < N_EXPERTS, logits, neg)
        m1 = jnp.max(logits, axis=-1, keepdims=True)
        i1 = jnp.min(jnp.where(logits == m1, lane, LANES), axis=-1, keepdims=True)
        rest_l = jnp.where(lane == i1, neg, logits)
        m2 = jnp.max(rest_l, axis=-1, keepdims=True)
        i2 = jnp.min(jnp.where(rest_l == m2, lane, LANES), axis=-1, keepdims=True)
        e2 = jnp.exp(m2 - m1)
        den = 1.0 + e2
        gates_ref[...] = jnp.where(lane == i1, 1.0 / den, 0.0) + jnp.where(lane == i2, e2 / den, 0.0)


def _mix_out_call(hs, u, yf, yb, agl, mod, p, router=None):
    nt = N_ROWS // TM
    hb = TM // HALO
    nh = N_ROWS // HALO
    row = lambda i: (i, 0)
    in_specs = [
        pl.BlockSpec((TM, D_MODEL), row),
        pl.BlockSpec((TM, D_SSM), row),
        pl.BlockSpec((TM, D_SSM), row),
        pl.BlockSpec((TM, D_SSM), row),
        pl.BlockSpec((TM, D_CONV), row),
        pl.BlockSpec((HALO, D_CONV), lambda i: (jnp.maximum(i * hb - 1, 0), 0)),
        pl.BlockSpec((HALO, D_CONV), lambda i: (jnp.minimum((i + 1) * hb, nh - 1), 0)),
        pl.BlockSpec((1, N_MOD, SUBLANES, D_MODEL), lambda i: (_seg_index(i, 0), 0, 0, 0)),
        _const_spec((1, D_SSM)),
        _const_spec((D_SSM, D_SSM)),
        _const_spec((1, D_SSM)),
        _const_spec((CONV_WIDTH, SUBLANES, D_CONV)),
        _const_spec((1, D_CONV)),
        _const_spec((1, D_CONV)),
        _const_spec((1, D_CONV)),
        _const_spec((D_MODEL, D_MODEL)),
        _const_spec((1, D_MODEL)),
        _const_spec((1, D_MODEL)),
    ]
    args = [hs, u, yf, yb, agl, agl, agl, mod, p["ssm_d"], p["w_glu"], p["b_glu"], p["conv_w"],
            p["conv_b"], p["ln_g"], p["ln_b"], p["w_out"], p["g_post_mix"], p["g_pre_ffn"]]
    out_specs = [pl.BlockSpec((TM, D_MODEL), row), pl.BlockSpec((TM, D_MODEL), row)]
    out_shape = [jax.ShapeDtypeStruct((N_ROWS, D_MODEL), F32),
                 jax.ShapeDtypeStruct((N_ROWS, D_MODEL), BF16)]
    if router is not None:
        in_specs += [_const_spec((D_MODEL, LANES)), _const_spec((1, LANES))]
        args += list(router)
        out_specs.append(pl.BlockSpec((TM, LANES), row))
        out_shape.append(jax.ShapeDtypeStruct((N_ROWS, LANES), F32))
    return pl.pallas_call(
        functools.partial(_mix_out_kernel, router is not None),
        grid=(nt,),
        in_specs=in_specs,
        out_specs=out_specs,
        out_shape=out_shape,
        scratch_shapes=[
            pltpu.VMEM((TM + 2 * CONV_PAD * BATCH, D_CONV), F32),
            pltpu.VMEM((TM, D_CONV), F32),
        ],
        compiler_params=_params(("arbitrary",)),
        name="mix_out",
    )(*args)


def _ffn_kernel(f_ref, h_ref, mod_ref, wg_ref, wu_ref, wd_ref, gpost_ref, o_ref):
    f = f_ref[...]
    acc = None
    for j in range(D_FF // FF_CHUNK):
        cols = slice(j * FF_CHUNK, (j + 1) * FF_CHUNK)
        g = _dot(f, wg_ref[:, cols])
        a = (g * jax.nn.sigmoid(g)) * _dot(f, wu_ref[:, cols])
        part = _dot(a.astype(BF16), wd_ref[cols, :])
        acc = part if acc is None else acc + part
    o_ref[...] = _tile_gate_add(h_ref[...], mod_ref[0, 5], _rms(acc, gpost_ref[...]))


def _ffn_call(f, hs, mod, wg, wu, wd, g_post):
    nt = N_ROWS // TM
    row = lambda i: (i, 0)
    once = dict(pipeline_mode=pl.Buffered(1))
    return pl.pallas_call(
        _ffn_kernel,
        grid=(nt,),
        in_specs=[
            pl.BlockSpec((TM, D_MODEL), row),
            pl.BlockSpec((TM, D_MODEL), row),
            pl.BlockSpec((1, N_MOD, SUBLANES, D_MODEL), lambda i: (_seg_index(i, 0), 0, 0, 0)),
            pl.BlockSpec((D_MODEL, D_FF), lambda i: (0, 0), **once),
            pl.BlockSpec((D_MODEL, D_FF), lambda i: (0, 0), **once),
            pl.BlockSpec((D_FF, D_MODEL), lambda i: (0, 0), **once),
            _const_spec((1, D_MODEL)),
        ],
        out_specs=pl.BlockSpec((TM, D_MODEL), row),
        out_shape=jax.ShapeDtypeStruct((N_ROWS, D_MODEL), F32),
        compiler_params=_params(("arbitrary",)),
        name="ffn_dense",
    )(f, hs, mod, wg, wu, wd, g_post)


def _moe_kernel(f_ref, h_ref, gates_ref, mod_ref, wg_ref, wu_ref, wd_ref, gpost_ref, o_ref, acc):
    e = pl.program_id(1)
    j = pl.program_id(2)

    @pl.when(jnp.logical_and(e == 0, j == 0))
    def _():
        acc[...] = jnp.zeros_like(acc)

    f = f_ref[...]
    gates = gates_ref[...]
    lane = lax.broadcasted_iota(jnp.int32, gates.shape, 1)
    gate_e = jnp.sum(jnp.where(lane == e, gates, 0.0), axis=-1, keepdims=True)
    g = _dot(f, wg_ref[0])
    a = (g * jax.nn.sigmoid(g)) * _dot(f, wu_ref[0])
    acc[...] += gate_e * _dot(a.astype(BF16), wd_ref[0])

    @pl.when(jnp.logical_and(e == N_EXPERTS - 1, j == pl.num_programs(2) - 1))
    def _():
        o_ref[...] = _tile_gate_add(h_ref[...], mod_ref[0, 5], _rms(acc[...], gpost_ref[...]))


def _moe_call(f, hs, gates, mod, wg, wu, wd, g_post, row0):
    off = row0 // MOE_TM
    nt = (N_ROWS - row0) // MOE_TM
    nf = D_FF // FF_CHUNK
    row = lambda i, e, j: (i + off, 0)
    seg = lambda i, e, j: (((i + off) >= (N_CTX // MOE_TM)).astype(jnp.int32), 0, 0, 0)
    return pl.pallas_call(
        _moe_kernel,
        grid=(nt, N_EXPERTS, nf),
        in_specs=[
            pl.BlockSpec((MOE_TM, D_MODEL), row),
            pl.BlockSpec((MOE_TM, D_MODEL), row),
            pl.BlockSpec((MOE_TM, LANES), row),
            pl.BlockSpec((1, N_MOD, SUBLANES, D_MODEL), seg),
            pl.BlockSpec((1, D_MODEL, FF_CHUNK), lambda i, e, j: (e, 0, j)),
            pl.BlockSpec((1, D_MODEL, FF_CHUNK), lambda i, e, j: (e, 0, j)),
            pl.BlockSpec((1, FF_CHUNK, D_MODEL), lambda i, e, j: (e, j, 0)),
            _const_spec((1, D_MODEL)),
        ],
        out_specs=pl.BlockSpec((MOE_TM, D_MODEL), lambda i, e, j: (i, 0)),
        out_shape=jax.ShapeDtypeStruct((N_ROWS - row0, D_MODEL), F32),
        scratch_shapes=[pltpu.VMEM((MOE_TM, D_MODEL), F32)],
        compiler_params=_params(("arbitrary", "arbitrary", "arbitrary")),
        name="moe_ffn",
    )(f, hs, gates, mod, wg, wu, wd, g_post)


def kernel(x, c, ctx, c_ctx, w_ada, b_ada, g_pre_mix, g_post_mix, g_pre_ffn, g_post_ffn, w_in,
           ssm_lam_re, ssm_lam_im, ssm_log_dt, ssm_b_re, ssm_b_im, ssm_c_re, ssm_c_im, ssm_d,
           ssm_w_glu, ssm_b_glu, conv_w, conv_b, conv_ln_g, conv_ln_b, w_out,
           ffn_w_gate, ffn_w_up, ffn_w_down, moe_w_router, moe_b_router, moe_w_gate, moe_w_up, moe_w_down):
    hs = jnp.concatenate([
        jnp.transpose(ctx, (1, 0, 2)).reshape(N_CTX, D_MODEL),
        jnp.transpose(x, (1, 0, 2)).reshape(N_LAT, D_MODEL)], axis=0)

    cin = jnp.concatenate([jnp.broadcast_to(c_ctx[None, :], (BATCH, D_MODEL)), c], axis=0)
    mod_all = _ada_call(cin, w_ada, b_ada)
    mod_all = mod_all.reshape(DEPTH, 2, BATCH, N_MOD, D_MODEL).transpose(0, 1, 3, 2, 4)

    row_vec = lambda v: v.reshape(1, -1).astype(F32)
    for l in range(DEPTH):
        mod = mod_all[l]
        u, agl = _mix_in_call(hs, mod, row_vec(g_pre_mix[l]), w_in[l].astype(BF16))
        yf, yb = _scan_call(u, *_s5_params(ssm_lam_re[l], ssm_lam_im[l], ssm_log_dt[l], ssm_b_re[l],
                                           ssm_b_im[l], ssm_c_re[l], ssm_c_im[l]))
        p = dict(
            ssm_d=row_vec(ssm_d[l]), w_glu=ssm_w_glu[l].astype(BF16), b_glu=row_vec(ssm_b_glu[l]),
            conv_w=jnp.broadcast_to(conv_w[l][:, None, :], (CONV_WIDTH, SUBLANES, D_CONV)).astype(F32),
            conv_b=row_vec(conv_b[l]), ln_g=row_vec(conv_ln_g[l]), ln_b=row_vec(conv_ln_b[l]),
            w_out=w_out[l].astype(BF16), g_post_mix=row_vec(g_post_mix[l]), g_pre_ffn=row_vec(g_pre_ffn[l]))
        i = l // 2
        if l % 2 == 0:
            hs, f = _mix_out_call(hs, u, yf, yb, agl, mod, p)
            hs = _ffn_call(f, hs, mod, ffn_w_gate[i].astype(BF16), ffn_w_up[i].astype(BF16),
                           ffn_w_down[i].astype(BF16), row_vec(g_post_ffn[l]))
        else:
            wr = jnp.zeros((D_MODEL, LANES), F32).at[:, :N_EXPERTS].set(moe_w_router[i].astype(F32))
            br = jnp.zeros((1, LANES), F32).at[0, :N_EXPERTS].set(moe_b_router[i].astype(F32))
            hs, f, gates = _mix_out_call(hs, u, yf, yb, agl, mod, p, router=(wr, br))
            row0 = 0 if l < DEPTH - 1 else N_CTX
            hs = _moe_call(f, hs, gates, mod, moe_w_gate[i].astype(BF16), moe_w_up[i].astype(BF16),
                           moe_w_down[i].astype(BF16), row_vec(g_post_ffn[l]), row0)
    return jnp.transpose(hs.reshape(SEQ, BATCH, D_MODEL), (1, 0, 2))
```

```python
import functools
import math

import jax
import jax.numpy as jnp
from jax import lax
from jax.experimental import pallas as pl
from jax.experimental.pallas import tpu as pltpu

F32 = jnp.float32
BF16 = jnp.bfloat16

D_MODEL = 1024
BATCH = 8
SEQ = 2048
CTX_LEN = 256
DEPTH = 4
D_SSM = 512
SSM_GROUP = 16
N_SSM_GROUPS = 32
SSM_STATE = 64
D_CONV = 512
CONV_WIDTH = 31
CONV_PAD = CONV_WIDTH // 2
D_IN = D_SSM + 2 * D_CONV
D_FF = 2816
N_EXPERTS = 8
N_MOD = 6
EPS = 1e-6

N_CTX = CTX_LEN * BATCH
N_LAT = SEQ * BATCH
N_ROWS = N_CTX + N_LAT
N_STATE = N_SSM_GROUPS * SSM_STATE

SUBLANES = 8
LANES = 128
VMEM_LIMIT = 56 * 1024 * 1024

TM = 512
SCAN_T = 64
SCAN_R = SCAN_T * BATCH
SCAN_SLABS = 2
SLAB_CH = D_SSM // SCAN_SLABS
SLAB_ST = N_STATE // SCAN_SLABS
SCAN_W = 512
HALO = 128
CONV_SUB = 32
FF_CHUNK = 1408
MOE_TM = 512
MOE_PAD = N_EXPERTS * MOE_TM
MOVE_BATCH = 1024
ROUTE_E1, ROUTE_E2, ROUTE_W1, ROUTE_W2 = 0, 1, 2, 3


def _dot(a, b):
    return jnp.dot(a, b, preferred_element_type=F32)


def _rms(x, g):
    ms = jnp.mean(x * x, axis=-1, keepdims=True)
    return x * lax.rsqrt(ms + EPS) * g


def _tile_mul_add(y, scale, shift):
    rows, d = y.shape
    y3 = y.reshape(rows // SUBLANES, SUBLANES, d)
    return (y3 * (1.0 + scale)[None] + shift[None]).reshape(rows, d)


def _tile_gate_add(h, gate, r):
    rows, d = h.shape
    r3 = r.reshape(rows // SUBLANES, SUBLANES, d)
    return h + (gate[None] * r3).reshape(rows, d)


def _params(sem):
    return pltpu.CompilerParams(dimension_semantics=sem, vmem_limit_bytes=VMEM_LIMIT)


def _const_spec(shape):
    nd = len(shape)
    return pl.BlockSpec(shape, lambda *_: (0,) * nd)


def _ada_kernel(c_ref, w_ref, b_ref, o_ref):
    c = c_ref[...]
    s = c * jax.nn.sigmoid(c)
    o_ref[0] = _dot(s, w_ref[0]) + b_ref[0]


def _ada_call(cin, w_ada, b_ada):
    tn = 1536
    return pl.pallas_call(
        _ada_kernel,
        grid=(DEPTH, N_MOD * D_MODEL // tn),
        in_specs=[
            pl.BlockSpec((2 * BATCH, D_MODEL), lambda l, j: (0, 0)),
            pl.BlockSpec((1, D_MODEL, tn), lambda l, j: (l, 0, j)),
            pl.BlockSpec((1, 1, tn), lambda l, j: (l, 0, j)),
        ],
        out_specs=pl.BlockSpec((1, 2 * BATCH, tn), lambda l, j: (l, 0, j)),
        out_shape=jax.ShapeDtypeStruct((DEPTH, 2 * BATCH, N_MOD * D_MODEL), F32),
        compiler_params=_params(("arbitrary", "arbitrary")),
        name="ada_mod",
    )(cin, w_ada, b_ada.reshape(DEPTH, 1, N_MOD * D_MODEL))


def _mix_in_kernel(h_ref, mod_ref, g_ref, w_ref, u_ref, a_ref):
    y = _rms(h_ref[...], g_ref[...])
    a = _tile_mul_add(y, mod_ref[0, 1], mod_ref[0, 0]).astype(BF16)
    p = _dot(a, w_ref[...])
    u_ref[...] = p[:, :D_SSM]
    v = p[:, D_SSM:D_SSM + D_CONV]
    g = p[:, D_SSM + D_CONV:]
    a_ref[...] = v * jax.nn.sigmoid(g)


def _seg_index(i, off):
    return ((i + off) >= (N_CTX // TM)).astype(jnp.int32)


def _mix_in_call(hs, mod, g_pre, w_in):
    nt = N_ROWS // TM
    return pl.pallas_call(
        _mix_in_kernel,
        grid=(nt,),
        in_specs=[
            pl.BlockSpec((TM, D_MODEL), lambda i: (i, 0)),
            pl.BlockSpec((1, N_MOD, SUBLANES, D_MODEL), lambda i: (_seg_index(i, 0), 0, 0, 0)),
            _const_spec((1, D_MODEL)),
            _const_spec((D_MODEL, D_IN)),
        ],
        out_specs=[
            pl.BlockSpec((TM, D_SSM), lambda i: (i, 0)),
            pl.BlockSpec((TM, D_CONV), lambda i: (i, 0)),
        ],
        out_shape=[
            jax.ShapeDtypeStruct((N_ROWS, D_SSM), F32),
            jax.ShapeDtypeStruct((N_ROWS, D_CONV), F32),
        ],
        compiler_params=_params(("arbitrary",)),
        name="mix_in",
    )(hs, mod, g_pre, w_in)


def _scan_kernel(uf_ref, ub_ref, bre_ref, bim_ref, cre_ref, cim_ref, lr_ref, li_ref,
                 yf_ref, yb_ref, hre, him, st_re, st_im):
    @pl.when(pl.program_id(0) == 0)
    def _():
        st_re[...] = jnp.zeros_like(st_re)
        st_im[...] = jnp.zeros_like(st_im)

    for d, u_ref in enumerate((uf_ref, ub_ref)):
        u = u_ref[...].astype(BF16)
        for s in range(SCAN_SLABS):
            us = u[:, s * SLAB_CH:(s + 1) * SLAB_CH]
            hre[d, :, s * SLAB_ST:(s + 1) * SLAB_ST] = _dot(us, bre_ref[d, s])
            him[d, :, s * SLAB_ST:(s + 1) * SLAB_ST] = _dot(us, bim_ref[d, s])

    for d in range(2):
        for c in range(N_STATE // SCAN_W):
            cols = slice(c * SCAN_W, (c + 1) * SCAN_W)
            lam_r = lr_ref[d, :, cols]
            lam_i = li_ref[d, :, cols]

            def step(t, carry, d=d, cols=cols, lam_r=lam_r, lam_i=lam_i):
                sr, si = carry
                tt = t if d == 0 else SCAN_T - 1 - t
                rows = pl.ds(pl.multiple_of(tt * SUBLANES, SUBLANES), SUBLANES)
                nr = lam_r * sr - lam_i * si + hre[d, rows, cols]
                ni = lam_r * si + lam_i * sr + him[d, rows, cols]
                hre[d, rows, cols] = nr
                him[d, rows, cols] = ni
                return nr, ni

            sr, si = lax.fori_loop(0, SCAN_T, step, (st_re[d, :, cols], st_im[d, :, cols]), unroll=8)
            st_re[d, :, cols] = sr
            st_im[d, :, cols] = si

    for d, y_ref in enumerate((yf_ref, yb_ref)):
        for s in range(SCAN_SLABS):
            st = slice(s * SLAB_ST, (s + 1) * SLAB_ST)
            y_ref[:, s * SLAB_CH:(s + 1) * SLAB_CH] = (
                _dot(hre[d, :, st].astype(BF16), cre_ref[d, s])
                + _dot(him[d, :, st].astype(BF16), cim_ref[d, s]))


def _bwd_chunk(k):
    nc = N_CTX // SCAN_R
    n = N_ROWS // SCAN_R
    return jnp.where(k < nc, nc - 1 - k, n - 1 + nc - k)


def _scan_call(u, bre, bim, cre, cim, lam_r, lam_i):
    n = N_ROWS // SCAN_R
    return pl.pallas_call(
        _scan_kernel,
        grid=(n,),
        in_specs=[
            pl.BlockSpec((SCAN_R, D_SSM), lambda k: (k, 0)),
            pl.BlockSpec((SCAN_R, D_SSM), lambda k: (_bwd_chunk(k), 0)),
            _const_spec((2, SCAN_SLABS, SLAB_CH, SLAB_ST)),
            _const_spec((2, SCAN_SLABS, SLAB_CH, SLAB_ST)),
            _const_spec((2, SCAN_SLABS, SLAB_ST, SLAB_CH)),
            _const_spec((2, SCAN_SLABS, SLAB_ST, SLAB_CH)),
            _const_spec((2, SUBLANES, N_STATE)),
            _const_spec((2, SUBLANES, N_STATE)),
        ],
        out_specs=[
            pl.BlockSpec((SCAN_R, D_SSM), lambda k: (k, 0)),
            pl.BlockSpec((SCAN_R, D_SSM), lambda k: (_bwd_chunk(k), 0)),
        ],
        out_shape=[
            jax.ShapeDtypeStruct((N_ROWS, D_SSM), F32),
            jax.ShapeDtypeStruct((N_ROWS, D_SSM), F32),
        ],
        scratch_shapes=[
            pltpu.VMEM((2, SCAN_R, N_STATE), F32),
            pltpu.VMEM((2, SCAN_R, N_STATE), F32),
            pltpu.VMEM((2, SUBLANES, N_STATE), F32),
            pltpu.VMEM((2, SUBLANES, N_STATE), F32),
        ],
        compiler_params=_params(("arbitrary",)),
        name="s5_scan",
    )(u, u, bre, bim, cre, cim, lam_r, lam_i)


def _s5_params(lam_re, lam_im, log_dt, b_re, b_im, c_re, c_im):
    lr, li = lam_re.astype(F32), lam_im.astype(F32)
    dt = jnp.exp(log_dt.astype(F32))[..., None]
    mag = jnp.exp(lr * dt)
    lbr, lbi = mag * jnp.cos(li * dt), mag * jnp.sin(li * dt)
    den = lr * lr + li * li
    qr = ((lbr - 1.0) * lr + lbi * li) / den
    qi = (lbi * lr - (lbr - 1.0) * li) / den
    br, bi = b_re.astype(F32), b_im.astype(F32)
    bbr = qr[..., None] * br - qi[..., None] * bi
    bbi = qr[..., None] * bi + qi[..., None] * br
    gs = N_SSM_GROUPS // SCAN_SLABS
    eye = jnp.eye(gs, dtype=F32)

    def b_blocks(m):
        m = m.reshape(2, SCAN_SLABS, gs, SSM_STATE, SSM_GROUP)
        out = jnp.einsum('dsgph,gk->dsghkp', m, eye)
        return out.reshape(2, SCAN_SLABS, SLAB_CH, SLAB_ST).astype(BF16)

    def c_blocks(m):
        m = m.reshape(2, SCAN_SLABS, gs, SSM_GROUP, SSM_STATE)
        out = jnp.einsum('dsghp,gk->dsgpkh', m, eye)
        return out.reshape(2, SCAN_SLABS, SLAB_ST, SLAB_CH).astype(BF16)

    def lanes(v):
        return jnp.broadcast_to(v.reshape(2, 1, N_STATE), (2, SUBLANES, N_STATE))

    return (b_blocks(bbr), b_blocks(bbi), c_blocks(c_re.astype(F32)), c_blocks(-c_im.astype(F32)),
            lanes(lbr), lanes(lbi))


def _mix_out_kernel(with_router, h_ref, u_ref, yf_ref, yb_ref, a_ref, ap_ref, an_ref, mod_ref,
                    dsk_ref, wglu_ref, bglu_ref, cw_ref, cb_ref, lng_ref, lnb_ref, wout_ref,
                    gpost_ref, gffn_ref, *rest):
    if with_router:
        wrh_ref, wrl_ref, br_ref, hn_ref, f_ref, route_ref, pad, cvs = rest
    else:
        hn_ref, f_ref, pad, cvs = rest
    i = pl.program_id(0)
    nct = N_CTX // TM
    nt = N_ROWS // TM
    first = jnp.logical_or(i == 0, i == nct)
    last = jnp.logical_or(i == nct - 1, i == nt - 1)
    edge = CONV_PAD * BATCH

    y = yf_ref[...] + yb_ref[...] + dsk_ref[...] * u_ref[...]
    z = jax.nn.gelu(y)
    s = z * jax.nn.sigmoid(_dot(z.astype(BF16), wglu_ref[...]) + bglu_ref[...])

    pad[0:edge, :] = jnp.where(first, 0.0, ap_ref[HALO - edge:HALO, :])
    pad[edge:edge + TM, :] = a_ref[...]
    pad[edge + TM:2 * edge + TM, :] = jnp.where(last, 0.0, an_ref[0:edge, :])

    def conv_block(j, carry):
        r0 = pl.multiple_of(j * CONV_SUB, CONV_SUB)
        acc = jnp.zeros((CONV_SUB // SUBLANES, SUBLANES, D_CONV), F32)
        for k in range(CONV_WIDTH):
            win = pad[pl.ds(r0 + k * BATCH, CONV_SUB), :]
            acc = acc + win.reshape(CONV_SUB // SUBLANES, SUBLANES, D_CONV) * cw_ref[k][None]
        cvs[pl.ds(r0, CONV_SUB), :] = acc.reshape(CONV_SUB, D_CONV)
        return carry

    lax.fori_loop(0, TM // CONV_SUB, conv_block, 0)
    cv = cvs[...] + cb_ref[...]
    mu = jnp.mean(cv, axis=-1, keepdims=True)
    xc = cv - mu
    var = jnp.mean(xc * xc, axis=-1, keepdims=True)
    cv = xc * lax.rsqrt(var + EPS) * lng_ref[...] + lnb_ref[...]
    cv = cv * jax.nn.sigmoid(cv)

    out = _dot(s.astype(BF16), wout_ref[0:D_SSM, :]) + _dot(cv.astype(BF16), wout_ref[D_SSM:, :])
    hn = _tile_gate_add(h_ref[...], mod_ref[0, 2], _rms(out, gpost_ref[...]))
    hn_ref[...] = hn
    f = _tile_mul_add(_rms(hn, gffn_ref[...]), mod_ref[0, 4], mod_ref[0, 3])
    f_ref[...] = f.astype(f_ref.dtype)

    if with_router:
        f_hi = f.astype(BF16)
        f_lo = (f - f_hi.astype(F32)).astype(BF16)
        logits = (_dot(f_hi, wrh_ref[...]) + _dot(f_lo, wrh_ref[...]) + _dot(f_hi, wrl_ref[...])
                  + br_ref[...])
        lane = lax.broadcasted_iota(jnp.int32, logits.shape, 1)
        neg = jnp.float32(-jnp.inf)
        logits = jnp.where(lane < N_EXPERTS, logits, neg)
        m1 = jnp.max(logits, axis=-1, keepdims=True)
        i1 = jnp.min(jnp.where(logits == m1, lane, LANES), axis=-1, keepdims=True)
        rest_l = jnp.where(lane == i1, neg, logits)
        m2 = jnp.max(rest_l, axis=-1, keepdims=True)
        i2 = jnp.min(jnp.where(rest_l == m2, lane, LANES), axis=-1, keepdims=True)
        e2 = jnp.exp(m2 - m1)
        den = 1.0 + e2
        route = jnp.where(lane == ROUTE_E1, i1.astype(F32), 0.0)
        route = jnp.where(lane == ROUTE_E2, i2.astype(F32), route)
        route = jnp.where(lane == ROUTE_W1, 1.0 / den, route)
        route_ref[...] = jnp.where(lane == ROUTE_W2, e2 / den, route)


def _mix_out_call(hs, u, yf, yb, agl, mod, p, router=None):
    nt = N_ROWS // TM
    hb = TM // HALO
    nh = N_ROWS // HALO
    row = lambda i: (i, 0)
    in_specs = [
        pl.BlockSpec((TM, D_MODEL), row),
        pl.BlockSpec((TM, D_SSM), row),
        pl.BlockSpec((TM, D_SSM), row),
        pl.BlockSpec((TM, D_SSM), row),
        pl.BlockSpec((TM, D_CONV), row),
        pl.BlockSpec((HALO, D_CONV), lambda i: (jnp.maximum(i * hb - 1, 0), 0)),
        pl.BlockSpec((HALO, D_CONV), lambda i: (jnp.minimum((i + 1) * hb, nh - 1), 0)),
        pl.BlockSpec((1, N_MOD, SUBLANES, D_MODEL), lambda i: (_seg_index(i, 0), 0, 0, 0)),
        _const_spec((1, D_SSM)),
        _const_spec((D_SSM, D_SSM)),
        _const_spec((1, D_SSM)),
        _const_spec((CONV_WIDTH, SUBLANES, D_CONV)),
        _const_spec((1, D_CONV)),
        _const_spec((1, D_CONV)),
        _const_spec((1, D_CONV)),
        _const_spec((D_MODEL, D_MODEL)),
        _const_spec((1, D_MODEL)),
        _const_spec((1, D_MODEL)),
    ]
    args = [hs, u, yf, yb, agl, agl, agl, mod, p["ssm_d"], p["w_glu"], p["b_glu"], p["conv_w"],
            p["conv_b"], p["ln_g"], p["ln_b"], p["w_out"], p["g_post_mix"], p["g_pre_ffn"]]
    out_specs = [pl.BlockSpec((TM, D_MODEL), row), pl.BlockSpec((TM, D_MODEL), row)]
    out_shape = [jax.ShapeDtypeStruct((N_ROWS, D_MODEL), F32),
                 jax.ShapeDtypeStruct((N_ROWS, D_MODEL), BF16 if router is None else F32)]
    if router is not None:
        in_specs += [_const_spec((D_MODEL, LANES)), _const_spec((D_MODEL, LANES)), _const_spec((1, LANES))]
        args += list(router)
        out_specs.append(pl.BlockSpec((TM, LANES), row))
        out_shape.append(jax.ShapeDtypeStruct((N_ROWS, LANES), F32))
    return pl.pallas_call(
        functools.partial(_mix_out_kernel, router is not None),
        grid=(nt,),
        in_specs=in_specs,
        out_specs=out_specs,
        out_shape=out_shape,
        scratch_shapes=[
            pltpu.VMEM((TM + 2 * CONV_PAD * BATCH, D_CONV), F32),
            pltpu.VMEM((TM, D_CONV), F32),
        ],
        compiler_params=_params(("arbitrary",)),
        name="mix_out",
    )(*args)


def _ffn_kernel(f_ref, h_ref, mod_ref, wg_ref, wu_ref, wd_ref, gpost_ref, o_ref):
    f = f_ref[...]
    acc = None
    for j in range(D_FF // FF_CHUNK):
        cols = slice(j * FF_CHUNK, (j + 1) * FF_CHUNK)
        g = _dot(f, wg_ref[:, cols])
        a = (g * jax.nn.sigmoid(g)) * _dot(f, wu_ref[:, cols])
        part = _dot(a.astype(BF16), wd_ref[cols, :])
        acc = part if acc is None else acc + part
    o_ref[...] = _tile_gate_add(h_ref[...], mod_ref[0, 5], _rms(acc, gpost_ref[...]))


def _ffn_call(f, hs, mod, wg, wu, wd, g_post):
    nt = N_ROWS // TM
    row = lambda i: (i, 0)
    once = dict(pipeline_mode=pl.Buffered(1))
    return pl.pallas_call(
        _ffn_kernel,
        grid=(nt,),
        in_specs=[
            pl.BlockSpec((TM, D_MODEL), row),
            pl.BlockSpec((TM, D_MODEL), row),
            pl.BlockSpec((1, N_MOD, SUBLANES, D_MODEL), lambda i: (_seg_index(i, 0), 0, 0, 0)),
            pl.BlockSpec((D_MODEL, D_FF), lambda i: (0, 0), **once),
            pl.BlockSpec((D_MODEL, D_FF), lambda i: (0, 0), **once),
            pl.BlockSpec((D_FF, D_MODEL), lambda i: (0, 0), **once),
            _const_spec((1, D_MODEL)),
        ],
        out_specs=pl.BlockSpec((TM, D_MODEL), row),
        out_shape=jax.ShapeDtypeStruct((N_ROWS, D_MODEL), F32),
        compiler_params=_params(("arbitrary",)),
        name="ffn_dense",
    )(f, hs, mod, wg, wu, wd, g_post)


def _dispatch_plan(route, row0):
    r = route[row0:]
    e = jnp.concatenate([r[:, ROUTE_E1], r[:, ROUTE_E2]]).astype(jnp.int32)
    ids = jnp.arange(N_EXPERTS, dtype=jnp.int32)
    onehot = (e[:, None] == ids[None, :]).astype(jnp.int32)
    csum = jnp.cumsum(onehot, axis=0)
    counts = csum[-1]
    rank = jnp.sum((csum - onehot) * onehot, axis=1)
    padded = ((counts + MOE_TM - 1) // MOE_TM) * MOE_TM
    ends = jnp.cumsum(padded)
    offs = ends - padded
    pos = jnp.sum(onehot * offs[None, :], axis=1) + rank

    padlen = padded - counts
    cp_end = jnp.cumsum(padlen)
    cp = cp_end - padlen
    j = jnp.arange(MOE_PAD, dtype=jnp.int32)
    ej = jnp.minimum(jnp.sum((j[:, None] >= cp_end[None, :]).astype(jnp.int32), axis=1), N_EXPERTS - 1)
    pad_pos = jnp.where(j < cp_end[-1], offs[ej] + counts[ej] + (j - cp[ej]), ends[-1] + (j - cp_end[-1]))

    n_tiles = (2 * r.shape[0] + MOE_PAD) // MOE_TM
    starts = jnp.arange(n_tiles, dtype=jnp.int32) * MOE_TM
    tile_expert = jnp.minimum(jnp.sum((starts[:, None] >= ends[None, :]).astype(jnp.int32), axis=1),
                              N_EXPERTS - 1)
    n_used = (ends[-1] // MOE_TM).reshape(1)
    return pos, pad_pos.astype(jnp.int32), tile_expert, n_used


def _row_move_kernel(idx_ref, src_ref, dst_ref, sem, *, rows_of):
    b = pl.program_id(0)

    def wait_batch():
        pltpu.make_async_copy(src_ref.at[pl.ds(0, MOVE_BATCH)], dst_ref.at[pl.ds(0, MOVE_BATCH)], sem).wait()

    def one(k, carry):
        s, d = rows_of(b * MOVE_BATCH + k, idx_ref)
        pltpu.make_async_copy(src_ref.at[pl.ds(s, 1)], dst_ref.at[pl.ds(d, 1)], sem).start()
        return carry

    lax.fori_loop(0, MOVE_BATCH, one, 0, unroll=8)

    @pl.when(b > 0)
    def _():
        wait_batch()

    @pl.when(b == pl.num_programs(0) - 1)
    def _():
        wait_batch()


def _row_move_call(idx, src, n_dst, n_rows, rows_of, name):
    return pl.pallas_call(
        functools.partial(_row_move_kernel, rows_of=rows_of),
        grid_spec=pltpu.PrefetchScalarGridSpec(
            num_scalar_prefetch=1,
            grid=(n_rows // MOVE_BATCH,),
            in_specs=[pl.BlockSpec(memory_space=pl.ANY)],
            out_specs=pl.BlockSpec(memory_space=pl.ANY),
            scratch_shapes=[pltpu.SemaphoreType.DMA(())],
        ),
        out_shape=jax.ShapeDtypeStruct((n_dst, src.shape[1]), src.dtype),
        compiler_params=_params(("arbitrary",)),
        name=name,
    )(idx, src)


def _moe_group_kernel(te_ref, nu_ref, x_ref, wg_ref, wu_ref, wd_ref, o_ref):
    p = pl.program_id(0)
    j = pl.program_id(1)

    @pl.when(p < nu_ref[0])
    def _():
        x = x_ref[...].astype(BF16)
        g = _dot(x, wg_ref[0])
        a = (g * jax.nn.sigmoid(g)) * _dot(x, wu_ref[0])
        part = _dot(a.astype(BF16), wd_ref[0])

        @pl.when(j == 0)
        def _():
            o_ref[...] = part

        @pl.when(j > 0)
        def _():
            o_ref[...] += part

    @pl.when(p >= nu_ref[0])
    def _():
        o_ref[...] = jnp.zeros_like(o_ref)


def _moe_group_call(tile_expert, n_used, xs, wg, wu, wd):
    n_tiles = xs.shape[0] // MOE_TM
    nf = D_FF // FF_CHUNK
    chunk = lambda p, j, nu: jnp.where(p < nu[0], j, nf - 1)
    return pl.pallas_call(
        _moe_group_kernel,
        grid_spec=pltpu.PrefetchScalarGridSpec(
            num_scalar_prefetch=2,
            grid=(n_tiles, nf),
            in_specs=[
                pl.BlockSpec((MOE_TM, D_MODEL), lambda p, j, te, nu: (p, 0)),
                pl.BlockSpec((1, D_MODEL, FF_CHUNK), lambda p, j, te, nu: (te[p], 0, chunk(p, j, nu))),
                pl.BlockSpec((1, D_MODEL, FF_CHUNK), lambda p, j, te, nu: (te[p], 0, chunk(p, j, nu))),
                pl.BlockSpec((1, FF_CHUNK, D_MODEL), lambda p, j, te, nu: (te[p], chunk(p, j, nu), 0)),
            ],
            out_specs=pl.BlockSpec((MOE_TM, D_MODEL), lambda p, j, te, nu: (p, 0)),
        ),
        out_shape=jax.ShapeDtypeStruct(xs.shape, F32),
        compiler_params=_params(("arbitrary", "arbitrary")),
        name="moe_group",
    )(tile_expert, n_used, xs, wg, wu, wd)


def _moe_combine_kernel(y1_ref, y2_ref, route_ref, h_ref, mod_ref, gpost_ref, o_ref):
    route = route_ref[...]
    lane = lax.broadcasted_iota(jnp.int32, route.shape, 1)
    w1 = jnp.sum(jnp.where(lane == ROUTE_W1, route, 0.0), axis=-1, keepdims=True)
    w2 = jnp.sum(jnp.where(lane == ROUTE_W2, route, 0.0), axis=-1, keepdims=True)
    y = w1 * y1_ref[...] + w2 * y2_ref[...]
    o_ref[...] = _tile_gate_add(h_ref[...], mod_ref[0, 5], _rms(y, gpost_ref[...]))


def _moe_combine_call(y_tok, route, hs, mod, g_post, row0):
    t = N_ROWS - row0
    nt = t // TM
    off = row0 // TM
    return pl.pallas_call(
        _moe_combine_kernel,
        grid=(nt,),
        in_specs=[
            pl.BlockSpec((TM, D_MODEL), lambda i: (i, 0)),
            pl.BlockSpec((TM, D_MODEL), lambda i: (i + nt, 0)),
            pl.BlockSpec((TM, LANES), lambda i: (i + off, 0)),
            pl.BlockSpec((TM, D_MODEL), lambda i: (i + off, 0)),
            pl.BlockSpec((1, N_MOD, SUBLANES, D_MODEL), lambda i: (_seg_index(i, off), 0, 0, 0)),
            _const_spec((1, D_MODEL)),
        ],
        out_specs=pl.BlockSpec((TM, D_MODEL), lambda i: (i, 0)),
        out_shape=jax.ShapeDtypeStruct((t, D_MODEL), F32),
        compiler_params=_params(("arbitrary",)),
        name="moe_combine",
    )(y_tok, y_tok, route, hs, mod, g_post)


def _moe_call(f, hs, route, mod, wg, wu, wd, g_post, row0):
    t = N_ROWS - row0
    pos, pad_pos, tile_expert, n_used = _dispatch_plan(route, row0)

    def to_sorted(a, idx):
        return row0 + jnp.minimum(a >> 1, t - 1), idx[a]

    def to_tokens(a, idx):
        return idx[a], a

    n_sorted = 2 * t + MOE_PAD
    pos_token_major = jnp.stack([pos[:t], pos[t:]], axis=1).reshape(2 * t)
    xs = _row_move_call(jnp.concatenate([pos_token_major, pad_pos]), f, n_sorted, n_sorted, to_sorted,
                        "moe_dispatch")
    ys = _moe_group_call(tile_expert, n_used, xs, wg, wu, wd)
    y_tok = _row_move_call(pos, ys, 2 * t, 2 * t, to_tokens, "moe_collect")
    return _moe_combine_call(y_tok, route, hs, mod, g_post, row0)


def kernel(x, c, ctx, c_ctx, w_ada, b_ada, g_pre_mix, g_post_mix, g_pre_ffn, g_post_ffn, w_in,
           ssm_lam_re, ssm_lam_im, ssm_log_dt, ssm_b_re, ssm_b_im, ssm_c_re, ssm_c_im, ssm_d,
           ssm_w_glu, ssm_b_glu, conv_w, conv_b, conv_ln_g, conv_ln_b, w_out,
           ffn_w_gate, ffn_w_up, ffn_w_down, moe_w_router, moe_b_router, moe_w_gate, moe_w_up, moe_w_down):
    hs = jnp.concatenate([
        jnp.transpose(ctx, (1, 0, 2)).reshape(N_CTX, D_MODEL),
        jnp.transpose(x, (1, 0, 2)).reshape(N_LAT, D_MODEL)], axis=0)

    cin = jnp.concatenate([jnp.broadcast_to(c_ctx[None, :], (BATCH, D_MODEL)), c], axis=0)
    mod_all = _ada_call(cin, w_ada, b_ada)
    mod_all = mod_all.reshape(DEPTH, 2, BATCH, N_MOD, D_MODEL).transpose(0, 1, 3, 2, 4)

    row_vec = lambda v: v.reshape(1, -1).astype(F32)
    for l in range(DEPTH):
        mod = mod_all[l]
        u, agl = _mix_in_call(hs, mod, row_vec(g_pre_mix[l]), w_in[l].astype(BF16))
        yf, yb = _scan_call(u, *_s5_params(ssm_lam_re[l], ssm_lam_im[l], ssm_log_dt[l], ssm_b_re[l],
                                           ssm_b_im[l], ssm_c_re[l], ssm_c_im[l]))
        p = dict(
            ssm_d=row_vec(ssm_d[l]), w_glu=ssm_w_glu[l].astype(BF16), b_glu=row_vec(ssm_b_glu[l]),
            conv_w=jnp.broadcast_to(conv_w[l][:, None, :], (CONV_WIDTH, SUBLANES, D_CONV)).astype(F32),
            conv_b=row_vec(conv_b[l]), ln_g=row_vec(conv_ln_g[l]), ln_b=row_vec(conv_ln_b[l]),
            w_out=w_out[l].astype(BF16), g_post_mix=row_vec(g_post_mix[l]), g_pre_ffn=row_vec(g_pre_ffn[l]))
        i = l // 2
        if l % 2 == 0:
            hs, f = _mix_out_call(hs, u, yf, yb, agl, mod, p)
            hs = _ffn_call(f, hs, mod, ffn_w_gate[i].astype(BF16), ffn_w_up[i].astype(BF16),
                           ffn_w_down[i].astype(BF16), row_vec(g_post_ffn[l]))
        else:
            wr = jnp.zeros((D_MODEL, LANES), F32).at[:, :N_EXPERTS].set(moe_w_router[i].astype(F32))
            wr_hi = wr.astype(BF16)
            wr_lo = (wr - wr_hi.astype(F32)).astype(BF16)
            br = jnp.zeros((1, LANES), F32).at[0, :N_EXPERTS].set(moe_b_router[i].astype(F32))
            hs, f, route = _mix_out_call(hs, u, yf, yb, agl, mod, p, router=(wr_hi, wr_lo, br))
            row0 = 0 if l < DEPTH - 1 else N_CTX
            hs = _moe_call(f, hs, route, mod, moe_w_gate[i].astype(BF16), moe_w_up[i].astype(BF16),
                           moe_w_down[i].astype(BF16), row_vec(g_post_ffn[l]), row0)
    return jnp.transpose(hs.reshape(SEQ, BATCH, D_MODEL), (1, 0, 2))
```

```python
import functools
import math

import jax
import jax.numpy as jnp
from jax import lax
from jax.experimental import pallas as pl
from jax.experimental.pallas import tpu as pltpu

F32 = jnp.float32
BF16 = jnp.bfloat16

D_MODEL = 1024
BATCH = 8
SEQ = 2048
CTX_LEN = 256
DEPTH = 4
D_SSM = 512
SSM_GROUP = 16
N_SSM_GROUPS = 32
SSM_STATE = 64
D_CONV = 512
CONV_WIDTH = 31
CONV_PAD = CONV_WIDTH // 2
D_IN = D_SSM + 2 * D_CONV
D_FF = 2816
N_EXPERTS = 8
N_MOD = 6
EPS = 1e-6

N_CTX = CTX_LEN * BATCH
N_LAT = SEQ * BATCH
N_ROWS = N_CTX + N_LAT
N_STATE = N_SSM_GROUPS * SSM_STATE

SUBLANES = 8
LANES = 128
VMEM_LIMIT = 56 * 1024 * 1024

TM = 512
SCAN_T = 64
SCAN_R = SCAN_T * BATCH
SCAN_SLABS = 2
SLAB_CH = D_SSM // SCAN_SLABS
SLAB_ST = N_STATE // SCAN_SLABS
SCAN_W = 512
HALO = 128
CONV_SUB = 32
FF_CHUNK = 1408
MOE_TM = 512
MOE_PAD = N_EXPERTS * MOE_TM
MOVE_BATCH = 1024
ROUTE_E1, ROUTE_E2, ROUTE_W1, ROUTE_W2 = 0, 1, 2, 3
TOKEN_TILE = (SUBLANES, D_MODEL // SUBLANES)


def _dot(a, b):
    return jnp.dot(a, b, preferred_element_type=F32)


def _rms(x, g):
    ms = jnp.mean(x * x, axis=-1, keepdims=True)
    return x * lax.rsqrt(ms + EPS) * g


def _tile_mul_add(y, scale, shift):
    rows, d = y.shape
    y3 = y.reshape(rows // SUBLANES, SUBLANES, d)
    return (y3 * (1.0 + scale)[None] + shift[None]).reshape(rows, d)


def _tile_gate_add(h, gate, r):
    rows, d = h.shape
    r3 = r.reshape(rows // SUBLANES, SUBLANES, d)
    return h + (gate[None] * r3).reshape(rows, d)


def _params(sem):
    return pltpu.CompilerParams(dimension_semantics=sem, vmem_limit_bytes=VMEM_LIMIT)


def _const_spec(shape):
    nd = len(shape)
    return pl.BlockSpec(shape, lambda *_: (0,) * nd)


def _ada_kernel(c_ref, w_ref, b_ref, o_ref):
    c = c_ref[...]
    s = c * jax.nn.sigmoid(c)
    o_ref[0] = _dot(s, w_ref[0]) + b_ref[0]


def _ada_call(cin, w_ada, b_ada):
    tn = 1536
    return pl.pallas_call(
        _ada_kernel,
        grid=(DEPTH, N_MOD * D_MODEL // tn),
        in_specs=[
            pl.BlockSpec((2 * BATCH, D_MODEL), lambda l, j: (0, 0)),
            pl.BlockSpec((1, D_MODEL, tn), lambda l, j: (l, 0, j)),
            pl.BlockSpec((1, 1, tn), lambda l, j: (l, 0, j)),
        ],
        out_specs=pl.BlockSpec((1, 2 * BATCH, tn), lambda l, j: (l, 0, j)),
        out_shape=jax.ShapeDtypeStruct((DEPTH, 2 * BATCH, N_MOD * D_MODEL), F32),
        compiler_params=_params(("arbitrary", "arbitrary")),
        name="ada_mod",
    )(cin, w_ada, b_ada.reshape(DEPTH, 1, N_MOD * D_MODEL))


def _to_stream(x):
    return pltpu.einshape("btd->tbd", x).reshape(x.shape[1] * BATCH, x.shape[2])


def _from_stream(h):
    return pltpu.einshape("tbd->btd", h.reshape(h.shape[0] // BATCH, BATCH, h.shape[1]))


def _mix_in_kernel(from_inputs, *refs):
    if from_inputs:
        ctx_ref, x_ref, mod_ref, g_ref, w_ref, u_ref, a_ref, hs_ref = refs
        is_ctx = pl.program_id(0) < N_CTX // TM

        @pl.when(is_ctx)
        def _():
            hs_ref[...] = _to_stream(ctx_ref[...])

        @pl.when(jnp.logical_not(is_ctx))
        def _():
            hs_ref[...] = _to_stream(x_ref[...])

        h = hs_ref[...]
    else:
        h_ref, mod_ref, g_ref, w_ref, u_ref, a_ref = refs
        h = h_ref[...]
    y = _rms(h, g_ref[...])
    a = _tile_mul_add(y, mod_ref[0, 1], mod_ref[0, 0]).astype(BF16)
    p = _dot(a, w_ref[...])
    u_ref[...] = p[:, :D_SSM]
    v = p[:, D_SSM:D_SSM + D_CONV]
    g = p[:, D_SSM + D_CONV:]
    a_ref[...] = v * jax.nn.sigmoid(g)


def _seg_index(i, off):
    return ((i + off) >= (N_CTX // TM)).astype(jnp.int32)


def _mix_in_call(hs, mod, g_pre, w_in):
    nt = N_ROWS // TM
    from_inputs = isinstance(hs, tuple)
    row = lambda i: (i, 0)
    if from_inputs:
        steps = TM // BATCH
        nct = N_CTX // TM
        h_specs = [pl.BlockSpec((BATCH, steps, D_MODEL), lambda i: (0, jnp.minimum(i, nct - 1), 0)),
                   pl.BlockSpec((BATCH, steps, D_MODEL), lambda i: (0, jnp.maximum(i - nct, 0), 0))]
        h_args = list(hs)
    else:
        h_specs = [pl.BlockSpec((TM, D_MODEL), row)]
        h_args = [hs]
    out_specs = [pl.BlockSpec((TM, D_SSM), row), pl.BlockSpec((TM, D_CONV), row)]
    out_shape = [jax.ShapeDtypeStruct((N_ROWS, D_SSM), F32), jax.ShapeDtypeStruct((N_ROWS, D_CONV), F32)]
    if from_inputs:
        out_specs.append(pl.BlockSpec((TM, D_MODEL), row))
        out_shape.append(jax.ShapeDtypeStruct((N_ROWS, D_MODEL), F32))
    return pl.pallas_call(
        functools.partial(_mix_in_kernel, from_inputs),
        grid=(nt,),
        in_specs=h_specs + [
            pl.BlockSpec((1, N_MOD, SUBLANES, D_MODEL), lambda i: (_seg_index(i, 0), 0, 0, 0)),
            _const_spec((1, D_MODEL)),
            _const_spec((D_MODEL, D_IN)),
        ],
        out_specs=out_specs,
        out_shape=out_shape,
        compiler_params=_params(("arbitrary",)),
        name="mix_in",
    )(*h_args, mod, g_pre, w_in)


def _scan_kernel(uf_ref, ub_ref, bre_ref, bim_ref, cre_ref, cim_ref, lr_ref, li_ref,
                 yf_ref, yb_ref, hre, him, st_re, st_im):
    @pl.when(pl.program_id(0) == 0)
    def _():
        st_re[...] = jnp.zeros_like(st_re)
        st_im[...] = jnp.zeros_like(st_im)

    for d, u_ref in enumerate((uf_ref, ub_ref)):
        u = u_ref[...].astype(BF16)
        for s in range(SCAN_SLABS):
            us = u[:, s * SLAB_CH:(s + 1) * SLAB_CH]
            hre[d, :, s * SLAB_ST:(s + 1) * SLAB_ST] = _dot(us, bre_ref[d, s])
            him[d, :, s * SLAB_ST:(s + 1) * SLAB_ST] = _dot(us, bim_ref[d, s])

    for d in range(2):
        for c in range(N_STATE // SCAN_W):
            cols = slice(c * SCAN_W, (c + 1) * SCAN_W)
            lam_r = lr_ref[d, :, cols]
            lam_i = li_ref[d, :, cols]

            def step(t, carry, d=d, cols=cols, lam_r=lam_r, lam_i=lam_i):
                sr, si = carry
                tt = t if d == 0 else SCAN_T - 1 - t
                rows = pl.ds(pl.multiple_of(tt * SUBLANES, SUBLANES), SUBLANES)
                nr = lam_r * sr - lam_i * si + hre[d, rows, cols]
                ni = lam_r * si + lam_i * sr + him[d, rows, cols]
                hre[d, rows, cols] = nr
                him[d, rows, cols] = ni
                return nr, ni

            sr, si = lax.fori_loop(0, SCAN_T, step, (st_re[d, :, cols], st_im[d, :, cols]), unroll=8)
            st_re[d, :, cols] = sr
            st_im[d, :, cols] = si

    for d, y_ref in enumerate((yf_ref, yb_ref)):
        for s in range(SCAN_SLABS):
            st = slice(s * SLAB_ST, (s + 1) * SLAB_ST)
            y_ref[:, s * SLAB_CH:(s + 1) * SLAB_CH] = (
                _dot(hre[d, :, st].astype(BF16), cre_ref[d, s])
                + _dot(him[d, :, st].astype(BF16), cim_ref[d, s]))


def _bwd_chunk(k):
    nc = N_CTX // SCAN_R
    n = N_ROWS // SCAN_R
    return jnp.where(k < nc, nc - 1 - k, n - 1 + nc - k)


def _scan_call(u, bre, bim, cre, cim, lam_r, lam_i):
    n = N_ROWS // SCAN_R
    return pl.pallas_call(
        _scan_kernel,
        grid=(n,),
        in_specs=[
            pl.BlockSpec((SCAN_R, D_SSM), lambda k: (k, 0)),
            pl.BlockSpec((SCAN_R, D_SSM), lambda k: (_bwd_chunk(k), 0)),
            _const_spec((2, SCAN_SLABS, SLAB_CH, SLAB_ST)),
            _const_spec((2, SCAN_SLABS, SLAB_CH, SLAB_ST)),
            _const_spec((2, SCAN_SLABS, SLAB_ST, SLAB_CH)),
            _const_spec((2, SCAN_SLABS, SLAB_ST, SLAB_CH)),
            _const_spec((2, SUBLANES, N_STATE)),
            _const_spec((2, SUBLANES, N_STATE)),
        ],
        out_specs=[
            pl.BlockSpec((SCAN_R, D_SSM), lambda k: (k, 0)),
            pl.BlockSpec((SCAN_R, D_SSM), lambda k: (_bwd_chunk(k), 0)),
        ],
        out_shape=[
            jax.ShapeDtypeStruct((N_ROWS, D_SSM), F32),
            jax.ShapeDtypeStruct((N_ROWS, D_SSM), F32),
        ],
        scratch_shapes=[
            pltpu.VMEM((2, SCAN_R, N_STATE), F32),
            pltpu.VMEM((2, SCAN_R, N_STATE), F32),
            pltpu.VMEM((2, SUBLANES, N_STATE), F32),
            pltpu.VMEM((2, SUBLANES, N_STATE), F32),
        ],
        compiler_params=_params(("arbitrary",)),
        name="s5_scan",
    )(u, u, bre, bim, cre, cim, lam_r, lam_i)


def _s5_params(lam_re, lam_im, log_dt, b_re, b_im, c_re, c_im):
    lr, li = lam_re.astype(F32), lam_im.astype(F32)
    dt = jnp.exp(log_dt.astype(F32))[..., None]
    mag = jnp.exp(lr * dt)
    lbr, lbi = mag * jnp.cos(li * dt), mag * jnp.sin(li * dt)
    den = lr * lr + li * li
    qr = ((lbr - 1.0) * lr + lbi * li) / den
    qi = (lbi * lr - (lbr - 1.0) * li) / den
    br, bi = b_re.astype(F32), b_im.astype(F32)
    bbr = qr[..., None] * br - qi[..., None] * bi
    bbi = qr[..., None] * bi + qi[..., None] * br
    gs = N_SSM_GROUPS // SCAN_SLABS
    eye = jnp.eye(gs, dtype=F32)

    def b_blocks(m):
        m = m.reshape(2, SCAN_SLABS, gs, SSM_STATE, SSM_GROUP)
        out = jnp.einsum('dsgph,gk->dsghkp', m, eye)
        return out.reshape(2, SCAN_SLABS, SLAB_CH, SLAB_ST).astype(BF16)

    def c_blocks(m):
        m = m.reshape(2, SCAN_SLABS, gs, SSM_GROUP, SSM_STATE)
        out = jnp.einsum('dsghp,gk->dsgpkh', m, eye)
        return out.reshape(2, SCAN_SLABS, SLAB_ST, SLAB_CH).astype(BF16)

    def lanes(v):
        return jnp.broadcast_to(v.reshape(2, 1, N_STATE), (2, SUBLANES, N_STATE))

    return (b_blocks(bbr), b_blocks(bbi), c_blocks(c_re.astype(F32)), c_blocks(-c_im.astype(F32)),
            lanes(lbr), lanes(lbi))


def _mix_out_kernel(with_router, h_ref, u_ref, yf_ref, yb_ref, a_ref, ap_ref, an_ref, mod_ref,
                    dsk_ref, wglu_ref, bglu_ref, cw_ref, cb_ref, lng_ref, lnb_ref, wout_ref,
                    gpost_ref, gffn_ref, *rest):
    if with_router:
        wrh_ref, wrl_ref, br_ref, hn_ref, f_ref, route_ref, pad, cvs = rest
    else:
        hn_ref, f_ref, pad, cvs = rest
    i = pl.program_id(0)
    nct = N_CTX // TM
    nt = N_ROWS // TM
    first = jnp.logical_or(i == 0, i == nct)
    last = jnp.logical_or(i == nct - 1, i == nt - 1)
    edge = CONV_PAD * BATCH

    y = yf_ref[...] + yb_ref[...] + dsk_ref[...] * u_ref[...]
    z = jax.nn.gelu(y)
    s = z * jax.nn.sigmoid(_dot(z.astype(BF16), wglu_ref[...]) + bglu_ref[...])

    pad[0:edge, :] = jnp.where(first, 0.0, ap_ref[HALO - edge:HALO, :])
    pad[edge:edge + TM, :] = a_ref[...]
    pad[edge + TM:2 * edge + TM, :] = jnp.where(last, 0.0, an_ref[0:edge, :])

    def conv_block(j, carry):
        r0 = pl.multiple_of(j * CONV_SUB, CONV_SUB)
        ng = CONV_SUB // BATCH
        step = lambda s: pad[pl.ds(r0 + s * BATCH, BATCH), :]
        win = [step(s) for s in range(ng)]
        acc = [jnp.zeros((BATCH, D_CONV), F32) for _ in range(ng)]
        for k in range(CONV_WIDTH):
            w = cw_ref[k]
            acc = [acc[q] + win[q] * w for q in range(ng)]
            if k + 1 < CONV_WIDTH:
                win = win[1:] + [step(k + ng)]
        for q in range(ng):
            cvs[pl.ds(r0 + q * BATCH, BATCH), :] = acc[q]
        return carry

    lax.fori_loop(0, TM // CONV_SUB, conv_block, 0)
    cv = cvs[...] + cb_ref[...]
    mu = jnp.mean(cv, axis=-1, keepdims=True)
    xc = cv - mu
    var = jnp.mean(xc * xc, axis=-1, keepdims=True)
    cv = xc * lax.rsqrt(var + EPS) * lng_ref[...] + lnb_ref[...]
    cv = cv * jax.nn.sigmoid(cv)

    out = _dot(s.astype(BF16), wout_ref[0:D_SSM, :]) + _dot(cv.astype(BF16), wout_ref[D_SSM:, :])
    hn = _tile_gate_add(h_ref[...], mod_ref[0, 2], _rms(out, gpost_ref[...]))
    hn_ref[...] = hn
    f = _tile_mul_add(_rms(hn, gffn_ref[...]), mod_ref[0, 4], mod_ref[0, 3])
    if with_router:
        f_ref[...] = f.reshape(f_ref.shape)
    else:
        f_ref[...] = f.astype(BF16)

    if with_router:
        f_hi = f.astype(BF16)
        f_lo = (f - f_hi.astype(F32)).astype(BF16)
        logits = (_dot(f_hi, wrh_ref[...]) + _dot(f_lo, wrh_ref[...]) + _dot(f_hi, wrl_ref[...])
                  + br_ref[...])
        lane = lax.broadcasted_iota(jnp.int32, logits.shape, 1)
        neg = jnp.float32(-jnp.inf)
        logits = jnp.where(lane < N_EXPERTS, logits, neg)
        m1 = jnp.max(logits, axis=-1, keepdims=True)
        i1 = jnp.min(jnp.where(logits == m1, lane, LANES), axis=-1, keepdims=True)
        rest_l = jnp.where(lane == i1, neg, logits)
        m2 = jnp.max(rest_l, axis=-1, keepdims=True)
        i2 = jnp.min(jnp.where(rest_l == m2, lane, LANES), axis=-1, keepdims=True)
        e2 = jnp.exp(m2 - m1)
        den = 1.0 + e2
        route = jnp.where(lane == ROUTE_E1, i1.astype(F32), 0.0)
        route = jnp.where(lane == ROUTE_E2, i2.astype(F32), route)
        route = jnp.where(lane == ROUTE_W1, 1.0 / den, route)
        route_ref[...] = jnp.where(lane == ROUTE_W2, e2 / den, route)


def _mix_out_call(hs, u, yf, yb, agl, mod, p, router=None):
    nt = N_ROWS // TM
    hb = TM // HALO
    nh = N_ROWS // HALO
    row = lambda i: (i, 0)
    in_specs = [
        pl.BlockSpec((TM, D_MODEL), row),
        pl.BlockSpec((TM, D_SSM), row),
        pl.BlockSpec((TM, D_SSM), row),
        pl.BlockSpec((TM, D_SSM), row),
        pl.BlockSpec((TM, D_CONV), row),
        pl.BlockSpec((HALO, D_CONV), lambda i: (jnp.maximum(i * hb - 1, 0), 0)),
        pl.BlockSpec((HALO, D_CONV), lambda i: (jnp.minimum((i + 1) * hb, nh - 1), 0)),
        pl.BlockSpec((1, N_MOD, SUBLANES, D_MODEL), lambda i: (_seg_index(i, 0), 0, 0, 0)),
        _const_spec((1, D_SSM)),
        _const_spec((D_SSM, D_SSM)),
        _const_spec((1, D_SSM)),
        _const_spec((CONV_WIDTH, SUBLANES, D_CONV)),
        _const_spec((1, D_CONV)),
        _const_spec((1, D_CONV)),
        _const_spec((1, D_CONV)),
        _const_spec((D_MODEL, D_MODEL)),
        _const_spec((1, D_MODEL)),
        _const_spec((1, D_MODEL)),
    ]
    args = [hs, u, yf, yb, agl, agl, agl, mod, p["ssm_d"], p["w_glu"], p["b_glu"], p["conv_w"],
            p["conv_b"], p["ln_g"], p["ln_b"], p["w_out"], p["g_post_mix"], p["g_pre_ffn"]]
    out_specs = [pl.BlockSpec((TM, D_MODEL), row)]
    out_shape = [jax.ShapeDtypeStruct((N_ROWS, D_MODEL), F32)]
    if router is None:
        out_specs.append(pl.BlockSpec((TM, D_MODEL), row))
        out_shape.append(jax.ShapeDtypeStruct((N_ROWS, D_MODEL), BF16))
    else:
        in_specs += [_const_spec((D_MODEL, LANES)), _const_spec((D_MODEL, LANES)), _const_spec((1, LANES))]
        args += list(router)
        out_specs += [pl.BlockSpec((TM,) + TOKEN_TILE, lambda i: (i, 0, 0)), pl.BlockSpec((TM, LANES), row)]
        out_shape += [jax.ShapeDtypeStruct((N_ROWS,) + TOKEN_TILE, F32),
                      jax.ShapeDtypeStruct((N_ROWS, LANES), F32)]
    return pl.pallas_call(
        functools.partial(_mix_out_kernel, router is not None),
        grid=(nt,),
        in_specs=in_specs,
        out_specs=out_specs,
        out_shape=out_shape,
        scratch_shapes=[
            pltpu.VMEM((TM + 2 * CONV_PAD * BATCH, D_CONV), F32),
            pltpu.VMEM((TM, D_CONV), F32),
        ],
        compiler_params=_params(("arbitrary",)),
        name="mix_out",
    )(*args)


def _ffn_kernel(f_ref, h_ref, mod_ref, wg_ref, wu_ref, wd_ref, gpost_ref, o_ref):
    f = f_ref[...]
    acc = None
    for j in range(D_FF // FF_CHUNK):
        cols = slice(j * FF_CHUNK, (j + 1) * FF_CHUNK)
        g = _dot(f, wg_ref[:, cols])
        a = (g * jax.nn.sigmoid(g)) * _dot(f, wu_ref[:, cols])
        part = _dot(a.astype(BF16), wd_ref[cols, :])
        acc = part if acc is None else acc + part
    o_ref[...] = _tile_gate_add(h_ref[...], mod_ref[0, 5], _rms(acc, gpost_ref[...]))


def _ffn_call(f, hs, mod, wg, wu, wd, g_post):
    nt = N_ROWS // TM
    row = lambda i: (i, 0)
    once = dict(pipeline_mode=pl.Buffered(1))
    return pl.pallas_call(
        _ffn_kernel,
        grid=(nt,),
        in_specs=[
            pl.BlockSpec((TM, D_MODEL), row),
            pl.BlockSpec((TM, D_MODEL), row),
            pl.BlockSpec((1, N_MOD, SUBLANES, D_MODEL), lambda i: (_seg_index(i, 0), 0, 0, 0)),
            pl.BlockSpec((D_MODEL, D_FF), lambda i: (0, 0), **once),
            pl.BlockSpec((D_MODEL, D_FF), lambda i: (0, 0), **once),
            pl.BlockSpec((D_FF, D_MODEL), lambda i: (0, 0), **once),
            _const_spec((1, D_MODEL)),
        ],
        out_specs=pl.BlockSpec((TM, D_MODEL), row),
        out_shape=jax.ShapeDtypeStruct((N_ROWS, D_MODEL), F32),
        compiler_params=_params(("arbitrary",)),
        name="ffn_dense",
    )(f, hs, mod, wg, wu, wd, g_post)


def _dispatch_plan(route, row0):
    r = route[row0:]
    e = jnp.concatenate([r[:, ROUTE_E1], r[:, ROUTE_E2]]).astype(jnp.int32)
    ids = jnp.arange(N_EXPERTS, dtype=jnp.int32)
    onehot = (e[:, None] == ids[None, :]).astype(jnp.int32)
    csum = jnp.cumsum(onehot, axis=0)
    counts = csum[-1]
    rank = jnp.sum((csum - onehot) * onehot, axis=1)
    padded = ((counts + MOE_TM - 1) // MOE_TM) * MOE_TM
    ends = jnp.cumsum(padded)
    offs = ends - padded
    pos = jnp.sum(onehot * offs[None, :], axis=1) + rank

    padlen = padded - counts
    cp_end = jnp.cumsum(padlen)
    cp = cp_end - padlen
    j = jnp.arange(MOE_PAD, dtype=jnp.int32)
    ej = jnp.minimum(jnp.sum((j[:, None] >= cp_end[None, :]).astype(jnp.int32), axis=1), N_EXPERTS - 1)
    pad_pos = jnp.where(j < cp_end[-1], offs[ej] + counts[ej] + (j - cp[ej]), ends[-1] + (j - cp_end[-1]))

    n_tiles = (2 * r.shape[0] + MOE_PAD) // MOE_TM
    starts = jnp.arange(n_tiles, dtype=jnp.int32) * MOE_TM
    tile_expert = jnp.minimum(jnp.sum((starts[:, None] >= ends[None, :]).astype(jnp.int32), axis=1),
                              N_EXPERTS - 1)
    n_used = (ends[-1] // MOE_TM).reshape(1)
    return pos, pad_pos.astype(jnp.int32), tile_expert, n_used


def _row_move_kernel(idx_ref, src_ref, dst_ref, sem, *, rows_of):
    b = pl.program_id(0)

    def wait_batch():
        pltpu.make_async_copy(src_ref.at[pl.ds(0, MOVE_BATCH)], dst_ref.at[pl.ds(0, MOVE_BATCH)], sem).wait()

    def one(k, carry):
        s, d = rows_of(b * MOVE_BATCH + k, idx_ref)
        pltpu.make_async_copy(src_ref.at[s], dst_ref.at[d], sem).start()
        return carry

    lax.fori_loop(0, MOVE_BATCH, one, 0, unroll=8)

    @pl.when(b > 0)
    def _():
        wait_batch()

    @pl.when(b == pl.num_programs(0) - 1)
    def _():
        wait_batch()


def _row_move_call(idx, src, n_dst, n_rows, rows_of, name):
    return pl.pallas_call(
        functools.partial(_row_move_kernel, rows_of=rows_of),
        grid_spec=pltpu.PrefetchScalarGridSpec(
            num_scalar_prefetch=1,
            grid=(n_rows // MOVE_BATCH,),
            in_specs=[pl.BlockSpec(memory_space=pl.ANY)],
            out_specs=pl.BlockSpec(memory_space=pl.ANY),
            scratch_shapes=[pltpu.SemaphoreType.DMA(())],
        ),
        out_shape=jax.ShapeDtypeStruct((n_dst,) + src.shape[1:], src.dtype),
        compiler_params=_params(("arbitrary",)),
        name=name,
    )(idx, src)


def _moe_group_kernel(te_ref, nu_ref, x_ref, wg_ref, wu_ref, wd_ref, o_ref):
    p = pl.program_id(0)
    j = pl.program_id(1)

    @pl.when(p < nu_ref[0])
    def _():
        x = x_ref[...].reshape(MOE_TM, D_MODEL).astype(BF16)
        g = _dot(x, wg_ref[0])
        a = (g * jax.nn.sigmoid(g)) * _dot(x, wu_ref[0])
        part = _dot(a.astype(BF16), wd_ref[0]).reshape(o_ref.shape)

        @pl.when(j == 0)
        def _():
            o_ref[...] = part

        @pl.when(j > 0)
        def _():
            o_ref[...] += part

    @pl.when(p >= nu_ref[0])
    def _():
        o_ref[...] = jnp.zeros_like(o_ref)


def _moe_group_call(tile_expert, n_used, xs, wg, wu, wd):
    n_tiles = xs.shape[0] // MOE_TM
    nf = D_FF // FF_CHUNK
    chunk = lambda p, j, nu: jnp.where(p < nu[0], j, nf - 1)
    return pl.pallas_call(
        _moe_group_kernel,
        grid_spec=pltpu.PrefetchScalarGridSpec(
            num_scalar_prefetch=2,
            grid=(n_tiles, nf),
            in_specs=[
                pl.BlockSpec((MOE_TM,) + TOKEN_TILE, lambda p, j, te, nu: (p, 0, 0)),
                pl.BlockSpec((1, D_MODEL, FF_CHUNK), lambda p, j, te, nu: (te[p], 0, chunk(p, j, nu))),
                pl.BlockSpec((1, D_MODEL, FF_CHUNK), lambda p, j, te, nu: (te[p], 0, chunk(p, j, nu))),
                pl.BlockSpec((1, FF_CHUNK, D_MODEL), lambda p, j, te, nu: (te[p], chunk(p, j, nu), 0)),
            ],
            out_specs=pl.BlockSpec((MOE_TM,) + TOKEN_TILE, lambda p, j, te, nu: (p, 0, 0)),
        ),
        out_shape=jax.ShapeDtypeStruct(xs.shape, F32),
        compiler_params=_params(("arbitrary", "arbitrary")),
        name="moe_group",
    )(tile_expert, n_used, xs, wg, wu, wd)


def _moe_combine_kernel(to_output, y1_ref, y2_ref, route_ref, h_ref, mod_ref, gpost_ref, o_ref):
    route = route_ref[...]
    lane = lax.broadcasted_iota(jnp.int32, route.shape, 1)
    w1 = jnp.sum(jnp.where(lane == ROUTE_W1, route, 0.0), axis=-1, keepdims=True)
    w2 = jnp.sum(jnp.where(lane == ROUTE_W2, route, 0.0), axis=-1, keepdims=True)
    y = w1 * y1_ref[...].reshape(TM, D_MODEL) + w2 * y2_ref[...].reshape(TM, D_MODEL)
    hn = _tile_gate_add(h_ref[...], mod_ref[0, 5], _rms(y, gpost_ref[...]))
    o_ref[...] = _from_stream(hn) if to_output else hn


def _moe_combine_call(y_tok, route, hs, mod, g_post, row0, to_output):
    t = N_ROWS - row0
    nt = t // TM
    off = row0 // TM
    if to_output:
        out_spec = pl.BlockSpec((BATCH, TM // BATCH, D_MODEL), lambda i: (0, i, 0))
        out_shape = jax.ShapeDtypeStruct((BATCH, t // BATCH, D_MODEL), F32)
    else:
        out_spec = pl.BlockSpec((TM, D_MODEL), lambda i: (i, 0))
        out_shape = jax.ShapeDtypeStruct((t, D_MODEL), F32)
    return pl.pallas_call(
        functools.partial(_moe_combine_kernel, to_output),
        grid=(nt,),
        in_specs=[
            pl.BlockSpec((TM,) + TOKEN_TILE, lambda i: (i, 0, 0)),
            pl.BlockSpec((TM,) + TOKEN_TILE, lambda i: (i + nt, 0, 0)),
            pl.BlockSpec((TM, LANES), lambda i: (i + off, 0)),
            pl.BlockSpec((TM, D_MODEL), lambda i: (i + off, 0)),
            pl.BlockSpec((1, N_MOD, SUBLANES, D_MODEL), lambda i: (_seg_index(i, off), 0, 0, 0)),
            _const_spec((1, D_MODEL)),
        ],
        out_specs=out_spec,
        out_shape=out_shape,
        compiler_params=_params(("arbitrary",)),
        name="moe_combine",
    )(y_tok, y_tok, route, hs, mod, g_post)


def _moe_call(f, hs, route, mod, wg, wu, wd, g_post, row0):
    t = N_ROWS - row0
    pos, pad_pos, tile_expert, n_used = _dispatch_plan(route, row0)

    def to_sorted(a, idx):
        return row0 + jnp.minimum(a >> 1, t - 1), idx[a]

    def to_tokens(a, idx):
        return idx[a], a

    n_sorted = 2 * t + MOE_PAD
    pos_token_major = jnp.stack([pos[:t], pos[t:]], axis=1).reshape(2 * t)
    xs = _row_move_call(jnp.concatenate([pos_token_major, pad_pos]), f, n_sorted, n_sorted, to_sorted,
                        "moe_dispatch")
    ys = _moe_group_call(tile_expert, n_used, xs, wg, wu, wd)
    y_tok = _row_move_call(pos, ys, 2 * t, 2 * t, to_tokens, "moe_collect")
    return _moe_combine_call(y_tok, route, hs, mod, g_post, row0, to_output=row0 > 0)


def kernel(x, c, ctx, c_ctx, w_ada, b_ada, g_pre_mix, g_post_mix, g_pre_ffn, g_post_ffn, w_in,
           ssm_lam_re, ssm_lam_im, ssm_log_dt, ssm_b_re, ssm_b_im, ssm_c_re, ssm_c_im, ssm_d,
           ssm_w_glu, ssm_b_glu, conv_w, conv_b, conv_ln_g, conv_ln_b, w_out,
           ffn_w_gate, ffn_w_up, ffn_w_down, moe_w_router, moe_b_router, moe_w_gate, moe_w_up, moe_w_down):
    cin = jnp.concatenate([jnp.broadcast_to(c_ctx[None, :], (BATCH, D_MODEL)), c], axis=0)
    mod_all = _ada_call(cin, w_ada, b_ada)
    mod_all = mod_all.reshape(DEPTH, 2, BATCH, N_MOD, D_MODEL).transpose(0, 1, 3, 2, 4)

    row_vec = lambda v: v.reshape(1, -1).astype(F32)
    for l in range(DEPTH):
        mod = mod_all[l]
        if l == 0:
            u, agl, hs = _mix_in_call((ctx, x), mod, row_vec(g_pre_mix[l]), w_in[l].astype(BF16))
        else:
            u, agl = _mix_in_call(hs, mod, row_vec(g_pre_mix[l]), w_in[l].astype(BF16))
        yf, yb = _scan_call(u, *_s5_params(ssm_lam_re[l], ssm_lam_im[l], ssm_log_dt[l], ssm_b_re[l],
                                           ssm_b_im[l], ssm_c_re[l], ssm_c_im[l]))
        p = dict(
            ssm_d=row_vec(ssm_d[l]), w_glu=ssm_w_glu[l].astype(BF16), b_glu=row_vec(ssm_b_glu[l]),
            conv_w=jnp.broadcast_to(conv_w[l][:, None, :], (CONV_WIDTH, SUBLANES, D_CONV)).astype(F32),
            conv_b=row_vec(conv_b[l]), ln_g=row_vec(conv_ln_g[l]), ln_b=row_vec(conv_ln_b[l]),
            w_out=w_out[l].astype(BF16), g_post_mix=row_vec(g_post_mix[l]), g_pre_ffn=row_vec(g_pre_ffn[l]))
        i = l // 2
        if l % 2 == 0:
            hs, f = _mix_out_call(hs, u, yf, yb, agl, mod, p)
            hs = _ffn_call(f, hs, mod, ffn_w_gate[i].astype(BF16), ffn_w_up[i].astype(BF16),
                           ffn_w_down[i].astype(BF16), row_vec(g_post_ffn[l]))
        else:
            wr = jnp.zeros((D_MODEL, LANES), F32).at[:, :N_EXPERTS].set(moe_w_router[i].astype(F32))
            wr_hi = wr.astype(BF16)
            wr_lo = (wr - wr_hi.astype(F32)).astype(BF16)
            br = jnp.zeros((1, LANES), F32).at[0, :N_EXPERTS].set(moe_b_router[i].astype(F32))
            hs, f, route = _mix_out_call(hs, u, yf, yb, agl, mod, p, router=(wr_hi, wr_lo, br))
            row0 = 0 if l < DEPTH - 1 else N_CTX
            hs = _moe_call(f, hs, route, mod, moe_w_gate[i].astype(BF16), moe_w_up[i].astype(BF16),
                           moe_w_down[i].astype(BF16), row_vec(g_post_ffn[l]), row0)
    return hs
```

```python
import functools
import math

import jax
import jax.numpy as jnp
from jax import lax
from jax.experimental import pallas as pl
from jax.experimental.pallas import tpu as pltpu

F32 = jnp.float32
BF16 = jnp.bfloat16

D_MODEL = 1024
BATCH = 8
SEQ = 2048
CTX_LEN = 256
DEPTH = 4
D_SSM = 512
SSM_GROUP = 16
N_SSM_GROUPS = 32
SSM_STATE = 64
D_CONV = 512
CONV_WIDTH = 31
CONV_PAD = CONV_WIDTH // 2
D_IN = D_SSM + 2 * D_CONV
D_FF = 2816
N_EXPERTS = 8
N_MOD = 6
EPS = 1e-6

N_CTX = CTX_LEN * BATCH
N_LAT = SEQ * BATCH
N_ROWS = N_CTX + N_LAT
N_STATE = N_SSM_GROUPS * SSM_STATE

SUBLANES = 8
LANES = 128
VMEM_LIMIT = 56 * 1024 * 1024

TM = 512
SCAN_T = 64
SCAN_R = SCAN_T * BATCH
SCAN_SLABS = 2
SLAB_CH = D_SSM // SCAN_SLABS
SLAB_ST = N_STATE // SCAN_SLABS
SCAN_W = 512
HALO = 128
CONV_SUB = 32
FF_CHUNK = 1408
MOE_TM = 512
MOE_PAD = N_EXPERTS * MOE_TM
ROUTE_E1, ROUTE_E2, ROUTE_W1, ROUTE_W2 = 0, 1, 2, 3
TOKEN_TILE = (SUBLANES, D_MODEL // SUBLANES)


def _dot(a, b):
    return jnp.dot(a, b, preferred_element_type=F32)


def _rms(x, g):
    ms = jnp.mean(x * x, axis=-1, keepdims=True)
    return x * lax.rsqrt(ms + EPS) * g


def _tile_mul_add(y, scale, shift):
    rows, d = y.shape
    y3 = y.reshape(rows // SUBLANES, SUBLANES, d)
    return (y3 * (1.0 + scale)[None] + shift[None]).reshape(rows, d)


def _tile_gate_add(h, gate, r):
    rows, d = h.shape
    r3 = r.reshape(rows // SUBLANES, SUBLANES, d)
    return h + (gate[None] * r3).reshape(rows, d)


def _params(sem):
    return pltpu.CompilerParams(dimension_semantics=sem, vmem_limit_bytes=VMEM_LIMIT)


def _const_spec(shape):
    nd = len(shape)
    return pl.BlockSpec(shape, lambda *_: (0,) * nd)


def _ada_kernel(c_ref, w_ref, b_ref, o_ref):
    c = c_ref[...]
    s = c * jax.nn.sigmoid(c)
    o_ref[0] = _dot(s, w_ref[0]) + b_ref[0]


def _ada_call(cin, w_ada, b_ada):
    tn = 1536
    return pl.pallas_call(
        _ada_kernel,
        grid=(DEPTH, N_MOD * D_MODEL // tn),
        in_specs=[
            pl.BlockSpec((2 * BATCH, D_MODEL), lambda l, j: (0, 0)),
            pl.BlockSpec((1, D_MODEL, tn), lambda l, j: (l, 0, j)),
            pl.BlockSpec((1, 1, tn), lambda l, j: (l, 0, j)),
        ],
        out_specs=pl.BlockSpec((1, 2 * BATCH, tn), lambda l, j: (l, 0, j)),
        out_shape=jax.ShapeDtypeStruct((DEPTH, 2 * BATCH, N_MOD * D_MODEL), F32),
        compiler_params=_params(("arbitrary", "arbitrary")),
        name="ada_mod",
    )(cin, w_ada, b_ada.reshape(DEPTH, 1, N_MOD * D_MODEL))


def _to_stream(x):
    return jnp.swapaxes(x, 0, 1).reshape(x.shape[1] * BATCH, x.shape[2])


def _from_stream(h):
    return jnp.swapaxes(h.reshape(h.shape[0] // BATCH, BATCH, h.shape[1]), 0, 1)


def _mix_in_kernel(from_inputs, *refs):
    if from_inputs:
        ctx_ref, x_ref, mod_ref, g_ref, w_ref, u_ref, a_ref, hs_ref = refs
        is_ctx = pl.program_id(0) < N_CTX // TM

        @pl.when(is_ctx)
        def _():
            hs_ref[...] = _to_stream(ctx_ref[...])

        @pl.when(jnp.logical_not(is_ctx))
        def _():
            hs_ref[...] = _to_stream(x_ref[...])

        h = hs_ref[...]
    else:
        h_ref, mod_ref, g_ref, w_ref, u_ref, a_ref = refs
        h = h_ref[...]
    y = _rms(h, g_ref[...])
    a = _tile_mul_add(y, mod_ref[0, 1], mod_ref[0, 0]).astype(BF16)
    p = _dot(a, w_ref[...])
    u_ref[...] = p[:, :D_SSM]
    v = p[:, D_SSM:D_SSM + D_CONV]
    g = p[:, D_SSM + D_CONV:]
    a_ref[...] = v * jax.nn.sigmoid(g)


def _seg_index(i, off):
    return ((i + off) >= (N_CTX // TM)).astype(jnp.int32)


def _mix_in_call(hs, mod, g_pre, w_in):
    nt = N_ROWS // TM
    from_inputs = isinstance(hs, tuple)
    row = lambda i: (i, 0)
    if from_inputs:
        steps = TM // BATCH
        nct = N_CTX // TM
        h_specs = [pl.BlockSpec((BATCH, steps, D_MODEL), lambda i: (0, jnp.minimum(i, nct - 1), 0)),
                   pl.BlockSpec((BATCH, steps, D_MODEL), lambda i: (0, jnp.maximum(i - nct, 0), 0))]
        h_args = list(hs)
    else:
        h_specs = [pl.BlockSpec((TM, D_MODEL), row)]
        h_args = [hs]
    out_specs = [pl.BlockSpec((TM, D_SSM), row), pl.BlockSpec((TM, D_CONV), row)]
    out_shape = [jax.ShapeDtypeStruct((N_ROWS, D_SSM), F32), jax.ShapeDtypeStruct((N_ROWS, D_CONV), F32)]
    if from_inputs:
        out_specs.append(pl.BlockSpec((TM, D_MODEL), row))
        out_shape.append(jax.ShapeDtypeStruct((N_ROWS, D_MODEL), F32))
    return pl.pallas_call(
        functools.partial(_mix_in_kernel, from_inputs),
        grid=(nt,),
        in_specs=h_specs + [
            pl.BlockSpec((1, N_MOD, SUBLANES, D_MODEL), lambda i: (_seg_index(i, 0), 0, 0, 0)),
            _const_spec((1, D_MODEL)),
            _const_spec((D_MODEL, D_IN)),
        ],
        out_specs=out_specs,
        out_shape=out_shape,
        compiler_params=_params(("arbitrary",)),
        name="mix_in",
    )(*h_args, mod, g_pre, w_in)


def _scan_kernel(uf_ref, ub_ref, bre_ref, bim_ref, cre_ref, cim_ref, lr_ref, li_ref,
                 yf_ref, yb_ref, hre, him, st_re, st_im):
    @pl.when(pl.program_id(0) == 0)
    def _():
        st_re[...] = jnp.zeros_like(st_re)
        st_im[...] = jnp.zeros_like(st_im)

    for d, u_ref in enumerate((uf_ref, ub_ref)):
        u = u_ref[...].astype(BF16)
        for s in range(SCAN_SLABS):
            us = u[:, s * SLAB_CH:(s + 1) * SLAB_CH]
            hre[d, :, s * SLAB_ST:(s + 1) * SLAB_ST] = _dot(us, bre_ref[d, s])
            him[d, :, s * SLAB_ST:(s + 1) * SLAB_ST] = _dot(us, bim_ref[d, s])

    for d in range(2):
        for c in range(N_STATE // SCAN_W):
            cols = slice(c * SCAN_W, (c + 1) * SCAN_W)
            lam_r = lr_ref[d, :, cols]
            lam_i = li_ref[d, :, cols]

            def step(t, carry, d=d, cols=cols, lam_r=lam_r, lam_i=lam_i):
                sr, si = carry
                tt = t if d == 0 else SCAN_T - 1 - t
                rows = pl.ds(pl.multiple_of(tt * SUBLANES, SUBLANES), SUBLANES)
                nr = lam_r * sr - lam_i * si + hre[d, rows, cols]
                ni = lam_r * si + lam_i * sr + him[d, rows, cols]
                hre[d, rows, cols] = nr
                him[d, rows, cols] = ni
                return nr, ni

            sr, si = lax.fori_loop(0, SCAN_T, step, (st_re[d, :, cols], st_im[d, :, cols]), unroll=8)
            st_re[d, :, cols] = sr
            st_im[d, :, cols] = si

    for d, y_ref in enumerate((yf_ref, yb_ref)):
        for s in range(SCAN_SLABS):
            st = slice(s * SLAB_ST, (s + 1) * SLAB_ST)
            y_ref[:, s * SLAB_CH:(s + 1) * SLAB_CH] = (
                _dot(hre[d, :, st].astype(BF16), cre_ref[d, s])
                + _dot(him[d, :, st].astype(BF16), cim_ref[d, s]))


def _bwd_chunk(k):
    nc = N_CTX // SCAN_R
    n = N_ROWS // SCAN_R
    return jnp.where(k < nc, nc - 1 - k, n - 1 + nc - k)


def _scan_call(u, bre, bim, cre, cim, lam_r, lam_i):
    n = N_ROWS // SCAN_R
    return pl.pallas_call(
        _scan_kernel,
        grid=(n,),
        in_specs=[
            pl.BlockSpec((SCAN_R, D_SSM), lambda k: (k, 0)),
            pl.BlockSpec((SCAN_R, D_SSM), lambda k: (_bwd_chunk(k), 0)),
            _const_spec((2, SCAN_SLABS, SLAB_CH, SLAB_ST)),
            _const_spec((2, SCAN_SLABS, SLAB_CH, SLAB_ST)),
            _const_spec((2, SCAN_SLABS, SLAB_ST, SLAB_CH)),
            _const_spec((2, SCAN_SLABS, SLAB_ST, SLAB_CH)),
            _const_spec((2, SUBLANES, N_STATE)),
            _const_spec((2, SUBLANES, N_STATE)),
        ],
        out_specs=[
            pl.BlockSpec((SCAN_R, D_SSM), lambda k: (k, 0)),
            pl.BlockSpec((SCAN_R, D_SSM), lambda k: (_bwd_chunk(k), 0)),
        ],
        out_shape=[
            jax.ShapeDtypeStruct((N_ROWS, D_SSM), F32),
            jax.ShapeDtypeStruct((N_ROWS, D_SSM), F32),
        ],
        scratch_shapes=[
            pltpu.VMEM((2, SCAN_R, N_STATE), F32),
            pltpu.VMEM((2, SCAN_R, N_STATE), F32),
            pltpu.VMEM((2, SUBLANES, N_STATE), F32),
            pltpu.VMEM((2, SUBLANES, N_STATE), F32),
        ],
        compiler_params=_params(("arbitrary",)),
        name="s5_scan",
    )(u, u, bre, bim, cre, cim, lam_r, lam_i)


def _s5_params(lam_re, lam_im, log_dt, b_re, b_im, c_re, c_im):
    lr, li = lam_re.astype(F32), lam_im.astype(F32)
    dt = jnp.exp(log_dt.astype(F32))[..., None]
    mag = jnp.exp(lr * dt)
    lbr, lbi = mag * jnp.cos(li * dt), mag * jnp.sin(li * dt)
    den = lr * lr + li * li
    qr = ((lbr - 1.0) * lr + lbi * li) / den
    qi = (lbi * lr - (lbr - 1.0) * li) / den
    br, bi = b_re.astype(F32), b_im.astype(F32)
    bbr = qr[..., None] * br - qi[..., None] * bi
    bbi = qr[..., None] * bi + qi[..., None] * br
    gs = N_SSM_GROUPS // SCAN_SLABS
    eye = jnp.eye(gs, dtype=F32)

    def b_blocks(m):
        m = m.reshape(2, SCAN_SLABS, gs, SSM_STATE, SSM_GROUP)
        out = jnp.einsum('dsgph,gk->dsghkp', m, eye)
        return out.reshape(2, SCAN_SLABS, SLAB_CH, SLAB_ST).astype(BF16)

    def c_blocks(m):
        m = m.reshape(2, SCAN_SLABS, gs, SSM_GROUP, SSM_STATE)
        out = jnp.einsum('dsghp,gk->dsgpkh', m, eye)
        return out.reshape(2, SCAN_SLABS, SLAB_ST, SLAB_CH).astype(BF16)

    def lanes(v):
        return jnp.broadcast_to(v.reshape(2, 1, N_STATE), (2, SUBLANES, N_STATE))

    return (b_blocks(bbr), b_blocks(bbi), c_blocks(c_re.astype(F32)), c_blocks(-c_im.astype(F32)),
            lanes(lbr), lanes(lbi))


def _mix_out_kernel(with_router, h_ref, u_ref, yf_ref, yb_ref, a_ref, ap_ref, an_ref, mod_ref,
                    dsk_ref, wglu_ref, bglu_ref, cw_ref, cb_ref, lng_ref, lnb_ref, wout_ref,
                    gpost_ref, gffn_ref, *rest):
    if with_router:
        wrh_ref, wrl_ref, br_ref, hn_ref, f_ref, route_ref, pad, cvs = rest
    else:
        hn_ref, f_ref, pad, cvs = rest
    i = pl.program_id(0)
    nct = N_CTX // TM
    nt = N_ROWS // TM
    first = jnp.logical_or(i == 0, i == nct)
    last = jnp.logical_or(i == nct - 1, i == nt - 1)
    edge = CONV_PAD * BATCH

    y = yf_ref[...] + yb_ref[...] + dsk_ref[...] * u_ref[...]
    z = jax.nn.gelu(y)
    s = z * jax.nn.sigmoid(_dot(z.astype(BF16), wglu_ref[...]) + bglu_ref[...])

    pad[0:edge, :] = jnp.where(first, 0.0, ap_ref[HALO - edge:HALO, :])
    pad[edge:edge + TM, :] = a_ref[...]
    pad[edge + TM:2 * edge + TM, :] = jnp.where(last, 0.0, an_ref[0:edge, :])

    def conv_block(j, carry):
        r0 = pl.multiple_of(j * CONV_SUB, CONV_SUB)
        ng = CONV_SUB // BATCH
        step = lambda s: pad[pl.ds(r0 + s * BATCH, BATCH), :]
        win = [step(s) for s in range(ng)]
        acc = [jnp.zeros((BATCH, D_CONV), F32) for _ in range(ng)]
        for k in range(CONV_WIDTH):
            w = cw_ref[k]
            acc = [acc[q] + win[q] * w for q in range(ng)]
            if k + 1 < CONV_WIDTH:
                win = win[1:] + [step(k + ng)]
        for q in range(ng):
            cvs[pl.ds(r0 + q * BATCH, BATCH), :] = acc[q]
        return carry

    lax.fori_loop(0, TM // CONV_SUB, conv_block, 0)
    cv = cvs[...] + cb_ref[...]
    mu = jnp.mean(cv, axis=-1, keepdims=True)
    xc = cv - mu
    var = jnp.mean(xc * xc, axis=-1, keepdims=True)
    cv = xc * lax.rsqrt(var + EPS) * lng_ref[...] + lnb_ref[...]
    cv = cv * jax.nn.sigmoid(cv)

    out = _dot(s.astype(BF16), wout_ref[0:D_SSM, :]) + _dot(cv.astype(BF16), wout_ref[D_SSM:, :])
    hn = _tile_gate_add(h_ref[...], mod_ref[0, 2], _rms(out, gpost_ref[...]))
    hn_ref[...] = hn
    f = _tile_mul_add(_rms(hn, gffn_ref[...]), mod_ref[0, 4], mod_ref[0, 3])
    if with_router:
        f_ref[...] = f.reshape(f_ref.shape)
    else:
        f_ref[...] = f.astype(BF16)

    if with_router:
        f_hi = f.astype(BF16)
        f_lo = (f - f_hi.astype(F32)).astype(BF16)
        logits = (_dot(f_hi, wrh_ref[...]) + _dot(f_lo, wrh_ref[...]) + _dot(f_hi, wrl_ref[...])
                  + br_ref[...])
        lane = lax.broadcasted_iota(jnp.int32, logits.shape, 1)
        neg = jnp.float32(-jnp.inf)
        logits = jnp.where(lane < N_EXPERTS, logits, neg)
        m1 = jnp.max(logits, axis=-1, keepdims=True)
        i1 = jnp.min(jnp.where(logits == m1, lane, LANES), axis=-1, keepdims=True)
        rest_l = jnp.where(lane == i1, neg, logits)
        m2 = jnp.max(rest_l, axis=-1, keepdims=True)
        i2 = jnp.min(jnp.where(rest_l == m2, lane, LANES), axis=-1, keepdims=True)
        e2 = jnp.exp(m2 - m1)
        den = 1.0 + e2
        route = jnp.where(lane == ROUTE_E1, i1.astype(F32), 0.0)
        route = jnp.where(lane == ROUTE_E2, i2.astype(F32), route)
        route = jnp.where(lane == ROUTE_W1, 1.0 / den, route)
        route_ref[...] = jnp.where(lane == ROUTE_W2, e2 / den, route)


def _mix_out_call(hs, u, yf, yb, agl, mod, p, router=None):
    nt = N_ROWS // TM
    hb = TM // HALO
    nh = N_ROWS // HALO
    row = lambda i: (i, 0)
    in_specs = [
        pl.BlockSpec((TM, D_MODEL), row),
        pl.BlockSpec((TM, D_SSM), row),
        pl.BlockSpec((TM, D_SSM), row),
        pl.BlockSpec((TM, D_SSM), row),
        pl.BlockSpec((TM, D_CONV), row),
        pl.BlockSpec((HALO, D_CONV), lambda i: (jnp.maximum(i * hb - 1, 0), 0)),
        pl.BlockSpec((HALO, D_CONV), lambda i: (jnp.minimum((i + 1) * hb, nh - 1), 0)),
        pl.BlockSpec((1, N_MOD, SUBLANES, D_MODEL), lambda i: (_seg_index(i, 0), 0, 0, 0)),
        _const_spec((1, D_SSM)),
        _const_spec((D_SSM, D_SSM)),
        _const_spec((1, D_SSM)),
        _const_spec((CONV_WIDTH, SUBLANES, D_CONV)),
        _const_spec((1, D_CONV)),
        _const_spec((1, D_CONV)),
        _const_spec((1, D_CONV)),
        _const_spec((D_MODEL, D_MODEL)),
        _const_spec((1, D_MODEL)),
        _const_spec((1, D_MODEL)),
    ]
    args = [hs, u, yf, yb, agl, agl, agl, mod, p["ssm_d"], p["w_glu"], p["b_glu"], p["conv_w"],
            p["conv_b"], p["ln_g"], p["ln_b"], p["w_out"], p["g_post_mix"], p["g_pre_ffn"]]
    out_specs = [pl.BlockSpec((TM, D_MODEL), row)]
    out_shape = [jax.ShapeDtypeStruct((N_ROWS, D_MODEL), F32)]
    if router is None:
        out_specs.append(pl.BlockSpec((TM, D_MODEL), row))
        out_shape.append(jax.ShapeDtypeStruct((N_ROWS, D_MODEL), BF16))
    else:
        in_specs += [_const_spec((D_MODEL, LANES)), _const_spec((D_MODEL, LANES)), _const_spec((1, LANES))]
        args += list(router)
        out_specs += [pl.BlockSpec((TM,) + TOKEN_TILE, lambda i: (i, 0, 0)), pl.BlockSpec((TM, LANES), row)]
        out_shape += [jax.ShapeDtypeStruct((N_ROWS,) + TOKEN_TILE, F32),
                      jax.ShapeDtypeStruct((N_ROWS, LANES), F32)]
    return pl.pallas_call(
        functools.partial(_mix_out_kernel, router is not None),
        grid=(nt,),
        in_specs=in_specs,
        out_specs=out_specs,
        out_shape=out_shape,
        scratch_shapes=[
            pltpu.VMEM((TM + 2 * CONV_PAD * BATCH, D_CONV), F32),
            pltpu.VMEM((TM, D_CONV), F32),
        ],
        compiler_params=_params(("arbitrary",)),
        name="mix_out",
    )(*args)


def _ffn_kernel(f_ref, h_ref, mod_ref, wg_ref, wu_ref, wd_ref, gpost_ref, o_ref):
    f = f_ref[...]
    acc = None
    for j in range(D_FF // FF_CHUNK):
        cols = slice(j * FF_CHUNK, (j + 1) * FF_CHUNK)
        g = _dot(f, wg_ref[:, cols])
        a = (g * jax.nn.sigmoid(g)) * _dot(f, wu_ref[:, cols])
        part = _dot(a.astype(BF16), wd_ref[cols, :])
        acc = part if acc is None else acc + part
    o_ref[...] = _tile_gate_add(h_ref[...], mod_ref[0, 5], _rms(acc, gpost_ref[...]))


def _ffn_call(f, hs, mod, wg, wu, wd, g_post):
    nt = N_ROWS // TM
    row = lambda i: (i, 0)
    once = dict(pipeline_mode=pl.Buffered(1))
    return pl.pallas_call(
        _ffn_kernel,
        grid=(nt,),
        in_specs=[
            pl.BlockSpec((TM, D_MODEL), row),
            pl.BlockSpec((TM, D_MODEL), row),
            pl.BlockSpec((1, N_MOD, SUBLANES, D_MODEL), lambda i: (_seg_index(i, 0), 0, 0, 0)),
            pl.BlockSpec((D_MODEL, D_FF), lambda i: (0, 0), **once),
            pl.BlockSpec((D_MODEL, D_FF), lambda i: (0, 0), **once),
            pl.BlockSpec((D_FF, D_MODEL), lambda i: (0, 0), **once),
            _const_spec((1, D_MODEL)),
        ],
        out_specs=pl.BlockSpec((TM, D_MODEL), row),
        out_shape=jax.ShapeDtypeStruct((N_ROWS, D_MODEL), F32),
        compiler_params=_params(("arbitrary",)),
        name="ffn_dense",
    )(f, hs, mod, wg, wu, wd, g_post)


def _dispatch_plan(route, row0):
    r = route[row0:]
    e = jnp.concatenate([r[:, ROUTE_E1], r[:, ROUTE_E2]]).astype(jnp.int32)
    ids = jnp.arange(N_EXPERTS, dtype=jnp.int32)
    onehot = (e[:, None] == ids[None, :]).astype(jnp.int32)
    csum = jnp.cumsum(onehot, axis=0)
    counts = csum[-1]
    rank = jnp.sum((csum - onehot) * onehot, axis=1)
    padded = ((counts + MOE_TM - 1) // MOE_TM) * MOE_TM
    ends = jnp.cumsum(padded)
    offs = ends - padded
    pos = jnp.sum(onehot * offs[None, :], axis=1) + rank

    padlen = padded - counts
    cp_end = jnp.cumsum(padlen)
    cp = cp_end - padlen
    j = jnp.arange(MOE_PAD, dtype=jnp.int32)
    ej = jnp.minimum(jnp.sum((j[:, None] >= cp_end[None, :]).astype(jnp.int32), axis=1), N_EXPERTS - 1)
    pad_pos = jnp.where(j < cp_end[-1], offs[ej] + counts[ej] + (j - cp[ej]), ends[-1] + (j - cp_end[-1]))

    n_tiles = (2 * r.shape[0] + MOE_PAD) // MOE_TM
    starts = jnp.arange(n_tiles, dtype=jnp.int32) * MOE_TM
    tile_expert = jnp.minimum(jnp.sum((starts[:, None] >= ends[None, :]).astype(jnp.int32), axis=1),
                              N_EXPERTS - 1)
    n_used = (ends[-1] // MOE_TM).reshape(1)
    return pos, pad_pos.astype(jnp.int32), tile_expert, n_used


def _dispatch_kernel(pos_ref, pad_ref, f_ref, xs_ref, sem, *, n_tok):
    i = pl.program_id(0)

    def wait_tile():
        pltpu.make_async_copy(f_ref, xs_ref.at[pl.ds(0, TM)], sem).wait()

    def scatter(k, carry):
        tok = i * TM + k
        pltpu.make_async_copy(f_ref.at[k], xs_ref.at[pos_ref[tok]], sem).start()
        pltpu.make_async_copy(f_ref.at[k], xs_ref.at[pos_ref[n_tok + tok]], sem).start()
        return carry

    lax.fori_loop(0, TM, scatter, 0, unroll=8)

    @pl.when(i == 0)
    def _():
        def fill(k, carry):
            pltpu.make_async_copy(f_ref.at[0], xs_ref.at[pad_ref[k]], sem).start()
            return carry
        lax.fori_loop(0, MOE_PAD, fill, 0, unroll=8)
        for _ in range(MOE_PAD // TM):
            wait_tile()

    wait_tile()
    wait_tile()


def _dispatch_call(pos, pad_pos, f, row0):
    n_tok = N_ROWS - row0
    off = row0 // TM
    return pl.pallas_call(
        functools.partial(_dispatch_kernel, n_tok=n_tok),
        grid_spec=pltpu.PrefetchScalarGridSpec(
            num_scalar_prefetch=2,
            grid=(n_tok // TM,),
            in_specs=[pl.BlockSpec((TM,) + TOKEN_TILE, lambda i, pos, pad: (i + off, 0, 0))],
            out_specs=pl.BlockSpec(memory_space=pl.ANY),
            scratch_shapes=[pltpu.SemaphoreType.DMA(())],
        ),
        out_shape=jax.ShapeDtypeStruct((2 * n_tok + MOE_PAD,) + TOKEN_TILE, F32),
        compiler_params=_params(("arbitrary",)),
        name="moe_dispatch",
    )(pos, pad_pos, f)


def _moe_group_kernel(te_ref, nu_ref, x_ref, wg_ref, wu_ref, wd_ref, o_ref):
    p = pl.program_id(0)
    j = pl.program_id(1)

    @pl.when(p < nu_ref[0])
    def _():
        x = x_ref[...].reshape(MOE_TM, D_MODEL).astype(BF16)
        g = _dot(x, wg_ref[0])
        a = (g * jax.nn.sigmoid(g)) * _dot(x, wu_ref[0])
        part = _dot(a.astype(BF16), wd_ref[0]).reshape(o_ref.shape)

        @pl.when(j == 0)
        def _():
            o_ref[...] = part

        @pl.when(j > 0)
        def _():
            o_ref[...] += part

    @pl.when(p >= nu_ref[0])
    def _():
        o_ref[...] = jnp.zeros_like(o_ref)


def _moe_group_call(tile_expert, n_used, xs, wg, wu, wd):
    n_tiles = xs.shape[0] // MOE_TM
    nf = D_FF // FF_CHUNK
    chunk = lambda p, j, nu: jnp.where(p < nu[0], j, nf - 1)
    return pl.pallas_call(
        _moe_group_kernel,
        grid_spec=pltpu.PrefetchScalarGridSpec(
            num_scalar_prefetch=2,
            grid=(n_tiles, nf),
            in_specs=[
                pl.BlockSpec((MOE_TM,) + TOKEN_TILE, lambda p, j, te, nu: (p, 0, 0)),
                pl.BlockSpec((1, D_MODEL, FF_CHUNK), lambda p, j, te, nu: (te[p], 0, chunk(p, j, nu))),
                pl.BlockSpec((1, D_MODEL, FF_CHUNK), lambda p, j, te, nu: (te[p], 0, chunk(p, j, nu))),
                pl.BlockSpec((1, FF_CHUNK, D_MODEL), lambda p, j, te, nu: (te[p], chunk(p, j, nu), 0)),
            ],
            out_specs=pl.BlockSpec((MOE_TM,) + TOKEN_TILE, lambda p, j, te, nu: (p, 0, 0)),
        ),
        out_shape=jax.ShapeDtypeStruct(xs.shape, F32),
        compiler_params=_params(("arbitrary", "arbitrary")),
        name="moe_group",
    )(tile_expert, n_used, xs, wg, wu, wd)


def _moe_combine_kernel(to_output, n_tok, pos_ref, ys_ref, route_ref, h_ref, mod_ref, gpost_ref, o_ref,
                        ybuf, sem):
    i = pl.program_id(0)

    def gather(k, carry):
        tok = i * TM + k
        pltpu.make_async_copy(ys_ref.at[pos_ref[tok]], ybuf.at[0, k], sem).start()
        pltpu.make_async_copy(ys_ref.at[pos_ref[n_tok + tok]], ybuf.at[1, k], sem).start()
        return carry

    lax.fori_loop(0, TM, gather, 0, unroll=8)
    for slot in range(2):
        pltpu.make_async_copy(ys_ref.at[pl.ds(0, TM)], ybuf.at[slot], sem).wait()

    route = route_ref[...]
    lane = lax.broadcasted_iota(jnp.int32, route.shape, 1)
    w1 = jnp.sum(jnp.where(lane == ROUTE_W1, route, 0.0), axis=-1, keepdims=True)
    w2 = jnp.sum(jnp.where(lane == ROUTE_W2, route, 0.0), axis=-1, keepdims=True)
    y = w1 * ybuf[0].reshape(TM, D_MODEL) + w2 * ybuf[1].reshape(TM, D_MODEL)
    hn = _tile_gate_add(h_ref[...], mod_ref[0, 5], _rms(y, gpost_ref[...]))
    o_ref[...] = _from_stream(hn) if to_output else hn


def _moe_combine_call(pos, ys, route, hs, mod, g_post, row0, to_output):
    t = N_ROWS - row0
    nt = t // TM
    off = row0 // TM
    if to_output:
        out_spec = pl.BlockSpec((BATCH, TM // BATCH, D_MODEL), lambda i, pos: (0, i, 0))
        out_shape = jax.ShapeDtypeStruct((BATCH, t // BATCH, D_MODEL), F32)
    else:
        out_spec = pl.BlockSpec((TM, D_MODEL), lambda i, pos: (i, 0))
        out_shape = jax.ShapeDtypeStruct((t, D_MODEL), F32)
    return pl.pallas_call(
        functools.partial(_moe_combine_kernel, to_output, t),
        grid_spec=pltpu.PrefetchScalarGridSpec(
            num_scalar_prefetch=1,
            grid=(nt,),
            in_specs=[
                pl.BlockSpec(memory_space=pl.ANY),
                pl.BlockSpec((TM, LANES), lambda i, pos: (i + off, 0)),
                pl.BlockSpec((TM, D_MODEL), lambda i, pos: (i + off, 0)),
                pl.BlockSpec((1, N_MOD, SUBLANES, D_MODEL), lambda i, pos: (_seg_index(i, off), 0, 0, 0)),
                pl.BlockSpec((1, D_MODEL), lambda i, pos: (0, 0)),
            ],
            out_specs=out_spec,
            scratch_shapes=[pltpu.VMEM((2, TM) + TOKEN_TILE, F32), pltpu.SemaphoreType.DMA(())],
        ),
        out_shape=out_shape,
        compiler_params=_params(("arbitrary",)),
        name="moe_combine",
    )(pos, ys, route, hs, mod, g_post)


def _moe_call(f, hs, route, mod, wg, wu, wd, g_post, row0):
    pos, pad_pos, tile_expert, n_used = _dispatch_plan(route, row0)
    xs = _dispatch_call(pos, pad_pos, f, row0)
    ys = _moe_group_call(tile_expert, n_used, xs, wg, wu, wd)
    return _moe_combine_call(pos, ys, route, hs, mod, g_post, row0, to_output=row0 > 0)


def kernel(x, c, ctx, c_ctx, w_ada, b_ada, g_pre_mix, g_post_mix, g_pre_ffn, g_post_ffn, w_in,
           ssm_lam_re, ssm_lam_im, ssm_log_dt, ssm_b_re, ssm_b_im, ssm_c_re, ssm_c_im, ssm_d,
           ssm_w_glu, ssm_b_glu, conv_w, conv_b, conv_ln_g, conv_ln_b, w_out,
           ffn_w_gate, ffn_w_up, ffn_w_down, moe_w_router, moe_b_router, moe_w_gate, moe_w_up, moe_w_down):
    cin = jnp.concatenate([jnp.broadcast_to(c_ctx[None, :], (BATCH, D_MODEL)), c], axis=0)
    mod_all = _ada_call(cin, w_ada, b_ada)
    mod_all = mod_all.reshape(DEPTH, 2, BATCH, N_MOD, D_MODEL).transpose(0, 1, 3, 2, 4)

    row_vec = lambda v: v.reshape(1, -1).astype(F32)
    for l in range(DEPTH):
        mod = mod_all[l]
        if l == 0:
            u, agl, hs = _mix_in_call((ctx, x), mod, row_vec(g_pre_mix[l]), w_in[l].astype(BF16))
        else:
            u, agl = _mix_in_call(hs, mod, row_vec(g_pre_mix[l]), w_in[l].astype(BF16))
        yf, yb = _scan_call(u, *_s5_params(ssm_lam_re[l], ssm_lam_im[l], ssm_log_dt[l], ssm_b_re[l],
                                           ssm_b_im[l], ssm_c_re[l], ssm_c_im[l]))
        p = dict(
            ssm_d=row_vec(ssm_d[l]), w_glu=ssm_w_glu[l].astype(BF16), b_glu=row_vec(ssm_b_glu[l]),
            conv_w=jnp.broadcast_to(conv_w[l][:, None, :], (CONV_WIDTH, SUBLANES, D_CONV)).astype(F32),
            conv_b=row_vec(conv_b[l]), ln_g=row_vec(conv_ln_g[l]), ln_b=row_vec(conv_ln_b[l]),
            w_out=w_out[l].astype(BF16), g_post_mix=row_vec(g_post_mix[l]), g_pre_ffn=row_vec(g_pre_ffn[l]))
        i = l // 2
        if l % 2 == 0:
            hs, f = _mix_out_call(hs, u, yf, yb, agl, mod, p)
            hs = _ffn_call(f, hs, mod, ffn_w_gate[i].astype(BF16), ffn_w_up[i].astype(BF16),
                           ffn_w_down[i].astype(BF16), row_vec(g_post_ffn[l]))
        else:
            wr = jnp.zeros((D_MODEL, LANES), F32).at[:, :N_EXPERTS].set(moe_w_router[i].astype(F32))
            wr_hi = wr.astype(BF16)
            wr_lo = (wr - wr_hi.astype(F32)).astype(BF16)
            br = jnp.zeros((1, LANES), F32).at[0, :N_EXPERTS].set(moe_b_router[i].astype(F32))
            hs, f, route = _mix_out_call(hs, u, yf, yb, agl, mod, p, router=(wr_hi, wr_lo, br))
            row0 = 0 if l < DEPTH - 1 else N_CTX
            hs = _moe_call(f, hs, route, mod, moe_w_gate[i].astype(BF16), moe_w_up[i].astype(BF16),
                           moe_w_down[i].astype(BF16), row_vec(g_post_ffn[l]), row0)
    return hs
```

```python
import functools
import math

import jax
import jax.numpy as jnp
from jax import lax
from jax.experimental import pallas as pl
from jax.experimental.pallas import tpu as pltpu

F32 = jnp.float32
BF16 = jnp.bfloat16

D_MODEL = 1024
BATCH = 8
SEQ = 2048
CTX_LEN = 256
DEPTH = 4
D_SSM = 512
SSM_GROUP = 16
N_SSM_GROUPS = 32
SSM_STATE = 64
D_CONV = 512
CONV_WIDTH = 31
CONV_PAD = CONV_WIDTH // 2
D_IN = D_SSM + 2 * D_CONV
D_FF = 2816
N_EXPERTS = 8
N_MOD = 6
EPS = 1e-6

N_CTX = CTX_LEN * BATCH
N_LAT = SEQ * BATCH
N_ROWS = N_CTX + N_LAT
N_STATE = N_SSM_GROUPS * SSM_STATE

SUBLANES = 8
LANES = 128
VMEM_LIMIT = 56 * 1024 * 1024

TM = 512
SCAN_T = 64
SCAN_R = SCAN_T * BATCH
SCAN_SLABS = 2
SLAB_CH = D_SSM // SCAN_SLABS
SLAB_ST = N_STATE // SCAN_SLABS
B_SLABS = 2
B_SLAB_CH = D_SSM // B_SLABS
B_SLAB_ST = N_STATE // B_SLABS
SCAN_W = 512
HALO = 128
CONV_SUB = 32
FF_CHUNK = 1408
MOE_TM = 512
MOE_PAD = N_EXPERTS * MOE_TM
ROUTE_E1, ROUTE_E2, ROUTE_W1, ROUTE_W2 = 0, 1, 2, 3
TOKEN_TILE = (SUBLANES, D_MODEL // SUBLANES)


def _dot(a, b):
    return jnp.dot(a, b, preferred_element_type=F32)


def _rms(x, g):
    ms = jnp.mean(x * x, axis=-1, keepdims=True)
    return x * lax.rsqrt(ms + EPS) * g


def _tile_mul_add(y, scale, shift):
    rows, d = y.shape
    y3 = y.reshape(rows // SUBLANES, SUBLANES, d)
    return (y3 * (1.0 + scale)[None] + shift[None]).reshape(rows, d)


def _tile_gate_add(h, gate, r):
    rows, d = h.shape
    r3 = r.reshape(rows // SUBLANES, SUBLANES, d)
    return h + (gate[None] * r3).reshape(rows, d)


def _params(sem):
    return pltpu.CompilerParams(dimension_semantics=sem, vmem_limit_bytes=VMEM_LIMIT)


def _const_spec(shape):
    nd = len(shape)
    return pl.BlockSpec(shape, lambda *_: (0,) * nd)


def _ada_kernel(c_ref, w_ref, b_ref, o_ref):
    c = c_ref[...]
    s = c * jax.nn.sigmoid(c)
    o_ref[0] = _dot(s, w_ref[0]) + b_ref[0]


def _ada_call(cin, w_ada, b_ada):
    tn = 1536
    return pl.pallas_call(
        _ada_kernel,
        grid=(DEPTH, N_MOD * D_MODEL // tn),
        in_specs=[
            pl.BlockSpec((2 * BATCH, D_MODEL), lambda l, j: (0, 0)),
            pl.BlockSpec((1, D_MODEL, tn), lambda l, j: (l, 0, j)),
            pl.BlockSpec((1, 1, tn), lambda l, j: (l, 0, j)),
        ],
        out_specs=pl.BlockSpec((1, 2 * BATCH, tn), lambda l, j: (l, 0, j)),
        out_shape=jax.ShapeDtypeStruct((DEPTH, 2 * BATCH, N_MOD * D_MODEL), F32),
        compiler_params=_params(("arbitrary", "arbitrary")),
        name="ada_mod",
    )(cin, w_ada, b_ada.reshape(DEPTH, 1, N_MOD * D_MODEL))


def _to_stream(x):
    return jnp.swapaxes(x, 0, 1).reshape(x.shape[1] * BATCH, x.shape[2])


def _from_stream(h):
    return jnp.swapaxes(h.reshape(h.shape[0] // BATCH, BATCH, h.shape[1]), 0, 1)


def _mix_in_kernel(from_inputs, *refs):
    if from_inputs:
        ctx_ref, x_ref, mod_ref, g_ref, w_ref, u_ref, a_ref, hs_ref = refs
        is_ctx = pl.program_id(0) < N_CTX // TM

        @pl.when(is_ctx)
        def _():
            hs_ref[...] = _to_stream(ctx_ref[...])

        @pl.when(jnp.logical_not(is_ctx))
        def _():
            hs_ref[...] = _to_stream(x_ref[...])

        h = hs_ref[...]
    else:
        h_ref, mod_ref, g_ref, w_ref, u_ref, a_ref = refs
        h = h_ref[...]
    y = _rms(h, g_ref[...])
    a = _tile_mul_add(y, mod_ref[0, 1], mod_ref[0, 0]).astype(BF16)
    p = _dot(a, w_ref[0])
    u_ref[...] = p[:, :D_SSM]
    v = p[:, D_SSM:D_SSM + D_CONV]
    g = p[:, D_SSM + D_CONV:]
    a_ref[...] = v * jax.nn.sigmoid(g)


def _seg_index(i, off):
    return ((i + off) >= (N_CTX // TM)).astype(jnp.int32)


def _layer_spec(shape, layer, **kwargs):
    return pl.BlockSpec((1,) + shape, lambda *_: (layer,) + (0,) * len(shape), **kwargs)


def _mix_in_call(hs, mod, g_pre, w_in, layer):
    nt = N_ROWS // TM
    from_inputs = isinstance(hs, tuple)
    row = lambda i: (i, 0)
    if from_inputs:
        steps = TM // BATCH
        nct = N_CTX // TM
        h_specs = [pl.BlockSpec((BATCH, steps, D_MODEL), lambda i: (0, jnp.minimum(i, nct - 1), 0)),
                   pl.BlockSpec((BATCH, steps, D_MODEL), lambda i: (0, jnp.maximum(i - nct, 0), 0))]
        h_args = list(hs)
    else:
        h_specs = [pl.BlockSpec((TM, D_MODEL), row)]
        h_args = [hs]
    out_specs = [pl.BlockSpec((TM, D_SSM), row), pl.BlockSpec((TM, D_CONV), row)]
    out_shape = [jax.ShapeDtypeStruct((N_ROWS, D_SSM), F32), jax.ShapeDtypeStruct((N_ROWS, D_CONV), F32)]
    if from_inputs:
        out_specs.append(pl.BlockSpec((TM, D_MODEL), row))
        out_shape.append(jax.ShapeDtypeStruct((N_ROWS, D_MODEL), F32))
    return pl.pallas_call(
        functools.partial(_mix_in_kernel, from_inputs),
        grid=(nt,),
        in_specs=h_specs + [
            pl.BlockSpec((1, N_MOD, SUBLANES, D_MODEL), lambda i: (_seg_index(i, 0), 0, 0, 0)),
            _const_spec((1, D_MODEL)),
            _layer_spec((D_MODEL, D_IN), layer),
        ],
        out_specs=out_specs,
        out_shape=out_shape,
        compiler_params=_params(("arbitrary",)),
        name="mix_in",
    )(*h_args, mod, g_pre, w_in)


def _scan_kernel(uf_ref, ub_ref, bre_ref, bim_ref, cre_ref, cim_ref, lr_ref, li_ref,
                 yf_ref, yb_ref, hre, him, st_re, st_im):
    @pl.when(pl.program_id(0) == 0)
    def _():
        st_re[...] = jnp.zeros_like(st_re)
        st_im[...] = jnp.zeros_like(st_im)

    for d, u_ref in enumerate((uf_ref, ub_ref)):
        u = u_ref[...].astype(BF16)
        for s in range(B_SLABS):
            us = u[:, s * B_SLAB_CH:(s + 1) * B_SLAB_CH]
            hre[d, :, s * B_SLAB_ST:(s + 1) * B_SLAB_ST] = _dot(us, bre_ref[0, d, s])
            him[d, :, s * B_SLAB_ST:(s + 1) * B_SLAB_ST] = _dot(us, bim_ref[0, d, s])

    for d in range(2):
        for c in range(N_STATE // SCAN_W):
            cols = slice(c * SCAN_W, (c + 1) * SCAN_W)
            lam_r = lr_ref[0, d, :, cols]
            lam_i = li_ref[0, d, :, cols]

            def step(t, carry, d=d, cols=cols, lam_r=lam_r, lam_i=lam_i):
                sr, si = carry
                tt = t if d == 0 else SCAN_T - 1 - t
                rows = pl.ds(pl.multiple_of(tt * SUBLANES, SUBLANES), SUBLANES)
                nr = lam_r * sr - lam_i * si + hre[d, rows, cols]
                ni = lam_r * si + lam_i * sr + him[d, rows, cols]
                hre[d, rows, cols] = nr
                him[d, rows, cols] = ni
                return nr, ni

            sr, si = lax.fori_loop(0, SCAN_T, step, (st_re[d, :, cols], st_im[d, :, cols]), unroll=8)
            st_re[d, :, cols] = sr
            st_im[d, :, cols] = si

    for d, y_ref in enumerate((yf_ref, yb_ref)):
        for s in range(SCAN_SLABS):
            st = slice(s * SLAB_ST, (s + 1) * SLAB_ST)
            y_ref[:, s * SLAB_CH:(s + 1) * SLAB_CH] = (
                _dot(hre[d, :, st].astype(BF16), cre_ref[0, d, s])
                + _dot(him[d, :, st].astype(BF16), cim_ref[0, d, s]))


def _bwd_chunk(k):
    nc = N_CTX // SCAN_R
    n = N_ROWS // SCAN_R
    return jnp.where(k < nc, nc - 1 - k, n - 1 + nc - k)


def _scan_call(u, layer, bre, bim, cre, cim, lam_r, lam_i):
    n = N_ROWS // SCAN_R
    return pl.pallas_call(
        _scan_kernel,
        grid=(n,),
        in_specs=[
            pl.BlockSpec((SCAN_R, D_SSM), lambda k: (k, 0)),
            pl.BlockSpec((SCAN_R, D_SSM), lambda k: (_bwd_chunk(k), 0)),
            _layer_spec((2, B_SLABS, B_SLAB_CH, B_SLAB_ST), layer),
            _layer_spec((2, B_SLABS, B_SLAB_CH, B_SLAB_ST), layer),
            _layer_spec((2, SCAN_SLABS, SLAB_ST, SLAB_CH), layer),
            _layer_spec((2, SCAN_SLABS, SLAB_ST, SLAB_CH), layer),
            _layer_spec((2, SUBLANES, N_STATE), layer),
            _layer_spec((2, SUBLANES, N_STATE), layer),
        ],
        out_specs=[
            pl.BlockSpec((SCAN_R, D_SSM), lambda k: (k, 0)),
            pl.BlockSpec((SCAN_R, D_SSM), lambda k: (_bwd_chunk(k), 0)),
        ],
        out_shape=[
            jax.ShapeDtypeStruct((N_ROWS, D_SSM), F32),
            jax.ShapeDtypeStruct((N_ROWS, D_SSM), F32),
        ],
        scratch_shapes=[
            pltpu.VMEM((2, SCAN_R, N_STATE), F32),
            pltpu.VMEM((2, SCAN_R, N_STATE), F32),
            pltpu.VMEM((2, SUBLANES, N_STATE), F32),
            pltpu.VMEM((2, SUBLANES, N_STATE), F32),
        ],
        compiler_params=_params(("arbitrary",)),
        name="s5_scan",
    )(u, u, bre, bim, cre, cim, lam_r, lam_i)


def _s5_params(lam_re, lam_im, log_dt, b_re, b_im, c_re, c_im):
    lr, li = lam_re.astype(F32), lam_im.astype(F32)
    dt = jnp.exp(log_dt.astype(F32))[..., None]
    mag = jnp.exp(lr * dt)
    lbr, lbi = mag * jnp.cos(li * dt), mag * jnp.sin(li * dt)
    den = lr * lr + li * li
    qr = ((lbr - 1.0) * lr + lbi * li) / den
    qi = (lbi * lr - (lbr - 1.0) * li) / den
    br, bi = b_re.astype(F32), b_im.astype(F32)
    bbr = qr[..., None] * br - qi[..., None] * bi
    bbi = qr[..., None] * bi + qi[..., None] * br
    def b_blocks(m):
        gs = N_SSM_GROUPS // B_SLABS
        m = m.reshape(2, B_SLABS, gs, SSM_STATE, SSM_GROUP)
        out = jnp.einsum('dsgph,gk->dsghkp', m, jnp.eye(gs, dtype=F32))
        return out.reshape(2, B_SLABS, B_SLAB_CH, B_SLAB_ST).astype(BF16)

    def c_blocks(m):
        gs = N_SSM_GROUPS // SCAN_SLABS
        m = m.reshape(2, SCAN_SLABS, gs, SSM_GROUP, SSM_STATE)
        out = jnp.einsum('dsghp,gk->dsgpkh', m, jnp.eye(gs, dtype=F32))
        return out.reshape(2, SCAN_SLABS, SLAB_ST, SLAB_CH).astype(BF16)

    def lanes(v):
        return jnp.broadcast_to(v.reshape(2, 1, N_STATE), (2, SUBLANES, N_STATE))

    return (b_blocks(bbr), b_blocks(bbi), c_blocks(c_re.astype(F32)), c_blocks(-c_im.astype(F32)),
            lanes(lbr), lanes(lbi))


def _mix_out_kernel(with_router, h_ref, u_ref, yf_ref, yb_ref, a_ref, ap_ref, an_ref, mod_ref,
                    dsk_ref, wglu_ref, bglu_ref, cw_ref, cb_ref, lng_ref, lnb_ref, wout_ref,
                    gpost_ref, gffn_ref, *rest):
    if with_router:
        wrh_ref, wrl_ref, br_ref, hn_ref, f_ref, route_ref, pad, cvs = rest
    else:
        hn_ref, f_ref, pad, cvs = rest
    i = pl.program_id(0)
    nct = N_CTX // TM
    nt = N_ROWS // TM
    first = jnp.logical_or(i == 0, i == nct)
    last = jnp.logical_or(i == nct - 1, i == nt - 1)
    edge = CONV_PAD * BATCH

    y = yf_ref[...] + yb_ref[...] + dsk_ref[...] * u_ref[...]
    z = jax.nn.gelu(y)
    s = z * jax.nn.sigmoid(_dot(z.astype(BF16), wglu_ref[0]) + bglu_ref[...])

    pad[0:edge, :] = jnp.where(first, 0.0, ap_ref[HALO - edge:HALO, :])
    pad[edge:edge + TM, :] = a_ref[...]
    pad[edge + TM:2 * edge + TM, :] = jnp.where(last, 0.0, an_ref[0:edge, :])

    def conv_block(j, carry):
        r0 = pl.multiple_of(j * CONV_SUB, CONV_SUB)
        ng = CONV_SUB // BATCH
        step = lambda s: pad[pl.ds(r0 + s * BATCH, BATCH), :]
        win = [step(s) for s in range(ng)]
        acc = [jnp.zeros((BATCH, D_CONV), F32) for _ in range(ng)]
        for k in range(CONV_WIDTH):
            w = cw_ref[k]
            acc = [acc[q] + win[q] * w for q in range(ng)]
            if k + 1 < CONV_WIDTH:
                win = win[1:] + [step(k + ng)]
        for q in range(ng):
            cvs[pl.ds(r0 + q * BATCH, BATCH), :] = acc[q]
        return carry

    lax.fori_loop(0, TM // CONV_SUB, conv_block, 0)
    cv = cvs[...] + cb_ref[...]
    mu = jnp.mean(cv, axis=-1, keepdims=True)
    xc = cv - mu
    var = jnp.mean(xc * xc, axis=-1, keepdims=True)
    cv = xc * lax.rsqrt(var + EPS) * lng_ref[...] + lnb_ref[...]
    cv = cv * jax.nn.sigmoid(cv)

    out = _dot(s.astype(BF16), wout_ref[0, 0:D_SSM, :]) + _dot(cv.astype(BF16), wout_ref[0, D_SSM:, :])
    hn = _tile_gate_add(h_ref[...], mod_ref[0, 2], _rms(out, gpost_ref[...]))
    hn_ref[...] = hn
    f = _tile_mul_add(_rms(hn, gffn_ref[...]), mod_ref[0, 4], mod_ref[0, 3])
    if with_router:
        f_ref[...] = f.reshape(f_ref.shape)
    else:
        f_ref[...] = f.astype(BF16)

    if with_router:
        f_hi = f.astype(BF16)
        f_lo = (f - f_hi.astype(F32)).astype(BF16)
        logits = (_dot(f_hi, wrh_ref[...]) + _dot(f_lo, wrh_ref[...]) + _dot(f_hi, wrl_ref[...])
                  + br_ref[...])
        lane = lax.broadcasted_iota(jnp.int32, logits.shape, 1)
        neg = jnp.float32(-jnp.inf)
        logits = jnp.where(lane < N_EXPERTS, logits, neg)
        m1 = jnp.max(logits, axis=-1, keepdims=True)
        i1 = jnp.min(jnp.where(logits == m1, lane, LANES), axis=-1, keepdims=True)
        rest_l = jnp.where(lane == i1, neg, logits)
        m2 = jnp.max(rest_l, axis=-1, keepdims=True)
        i2 = jnp.min(jnp.where(rest_l == m2, lane, LANES), axis=-1, keepdims=True)
        e2 = jnp.exp(m2 - m1)
        den = 1.0 + e2
        route = jnp.where(lane == ROUTE_E1, i1.astype(F32), 0.0)
        route = jnp.where(lane == ROUTE_E2, i2.astype(F32), route)
        route = jnp.where(lane == ROUTE_W1, 1.0 / den, route)
        route_ref[...] = jnp.where(lane == ROUTE_W2, e2 / den, route)


def _mix_out_call(hs, u, yf, yb, agl, mod, p, layer, router=None):
    nt = N_ROWS // TM
    hb = TM // HALO
    nh = N_ROWS // HALO
    row = lambda i: (i, 0)
    in_specs = [
        pl.BlockSpec((TM, D_MODEL), row),
        pl.BlockSpec((TM, D_SSM), row),
        pl.BlockSpec((TM, D_SSM), row),
        pl.BlockSpec((TM, D_SSM), row),
        pl.BlockSpec((TM, D_CONV), row),
        pl.BlockSpec((HALO, D_CONV), lambda i: (jnp.maximum(i * hb - 1, 0), 0)),
        pl.BlockSpec((HALO, D_CONV), lambda i: (jnp.minimum((i + 1) * hb, nh - 1), 0)),
        pl.BlockSpec((1, N_MOD, SUBLANES, D_MODEL), lambda i: (_seg_index(i, 0), 0, 0, 0)),
        _const_spec((1, D_SSM)),
        _layer_spec((D_SSM, D_SSM), layer),
        _const_spec((1, D_SSM)),
        _const_spec((CONV_WIDTH, SUBLANES, D_CONV)),
        _const_spec((1, D_CONV)),
        _const_spec((1, D_CONV)),
        _const_spec((1, D_CONV)),
        _layer_spec((D_MODEL, D_MODEL), layer),
        _const_spec((1, D_MODEL)),
        _const_spec((1, D_MODEL)),
    ]
    args = [hs, u, yf, yb, agl, agl, agl, mod, p["ssm_d"], p["w_glu"], p["b_glu"], p["conv_w"],
            p["conv_b"], p["ln_g"], p["ln_b"], p["w_out"], p["g_post_mix"], p["g_pre_ffn"]]
    out_specs = [pl.BlockSpec((TM, D_MODEL), row)]
    out_shape = [jax.ShapeDtypeStruct((N_ROWS, D_MODEL), F32)]
    if router is None:
        out_specs.append(pl.BlockSpec((TM, D_MODEL), row))
        out_shape.append(jax.ShapeDtypeStruct((N_ROWS, D_MODEL), BF16))
    else:
        in_specs += [_const_spec((D_MODEL, LANES)), _const_spec((D_MODEL, LANES)), _const_spec((1, LANES))]
        args += list(router)
        out_specs += [pl.BlockSpec((TM,) + TOKEN_TILE, lambda i: (i, 0, 0)), pl.BlockSpec((TM, LANES), row)]
        out_shape += [jax.ShapeDtypeStruct((N_ROWS,) + TOKEN_TILE, F32),
                      jax.ShapeDtypeStruct((N_ROWS, LANES), F32)]
    return pl.pallas_call(
        functools.partial(_mix_out_kernel, router is not None),
        grid=(nt,),
        in_specs=in_specs,
        out_specs=out_specs,
        out_shape=out_shape,
        scratch_shapes=[
            pltpu.VMEM((TM + 2 * CONV_PAD * BATCH, D_CONV), F32),
            pltpu.VMEM((TM, D_CONV), F32),
        ],
        compiler_params=_params(("arbitrary",)),
        name="mix_out",
    )(*args)


def _ffn_kernel(f_ref, h_ref, mod_ref, wg_ref, wu_ref, wd_ref, gpost_ref, o_ref):
    f = f_ref[...]
    acc = None
    for j in range(D_FF // FF_CHUNK):
        cols = slice(j * FF_CHUNK, (j + 1) * FF_CHUNK)
        g = _dot(f, wg_ref[0, :, cols])
        a = (g * jax.nn.sigmoid(g)) * _dot(f, wu_ref[0, :, cols])
        part = _dot(a.astype(BF16), wd_ref[0, cols, :])
        acc = part if acc is None else acc + part
    o_ref[...] = _tile_gate_add(h_ref[...], mod_ref[0, 5], _rms(acc, gpost_ref[...]))


def _ffn_call(f, hs, mod, wg, wu, wd, g_post, layer):
    nt = N_ROWS // TM
    row = lambda i: (i, 0)
    once = dict(pipeline_mode=pl.Buffered(1))
    return pl.pallas_call(
        _ffn_kernel,
        grid=(nt,),
        in_specs=[
            pl.BlockSpec((TM, D_MODEL), row),
            pl.BlockSpec((TM, D_MODEL), row),
            pl.BlockSpec((1, N_MOD, SUBLANES, D_MODEL), lambda i: (_seg_index(i, 0), 0, 0, 0)),
            _layer_spec((D_MODEL, D_FF), layer, **once),
            _layer_spec((D_MODEL, D_FF), layer, **once),
            _layer_spec((D_FF, D_MODEL), layer, **once),
            _const_spec((1, D_MODEL)),
        ],
        out_specs=pl.BlockSpec((TM, D_MODEL), row),
        out_shape=jax.ShapeDtypeStruct((N_ROWS, D_MODEL), F32),
        compiler_params=_params(("arbitrary",)),
        name="ffn_dense",
    )(f, hs, mod, wg, wu, wd, g_post)


def _dispatch_plan(route, row0):
    r = route[row0:]
    e = jnp.concatenate([r[:, ROUTE_E1], r[:, ROUTE_E2]]).astype(jnp.int32)
    ids = jnp.arange(N_EXPERTS, dtype=jnp.int32)
    onehot = (e[:, None] == ids[None, :]).astype(jnp.int32)
    csum = jnp.cumsum(onehot, axis=0)
    counts = csum[-1]
    rank = jnp.sum((csum - onehot) * onehot, axis=1)
    padded = ((counts + MOE_TM - 1) // MOE_TM) * MOE_TM
    ends = jnp.cumsum(padded)
    offs = ends - padded
    pos = jnp.sum(onehot * offs[None, :], axis=1) + rank

    padlen = padded - counts
    cp_end = jnp.cumsum(padlen)
    cp = cp_end - padlen
    j = jnp.arange(MOE_PAD, dtype=jnp.int32)
    ej = jnp.minimum(jnp.sum((j[:, None] >= cp_end[None, :]).astype(jnp.int32), axis=1), N_EXPERTS - 1)
    pad_pos = jnp.where(j < cp_end[-1], offs[ej] + counts[ej] + (j - cp[ej]), ends[-1] + (j - cp_end[-1]))

    n_tiles = (2 * r.shape[0] + MOE_PAD) // MOE_TM
    starts = jnp.arange(n_tiles, dtype=jnp.int32) * MOE_TM
    tile_expert = jnp.minimum(jnp.sum((starts[:, None] >= ends[None, :]).astype(jnp.int32), axis=1),
                              N_EXPERTS - 1)
    n_used = (ends[-1] // MOE_TM).reshape(1)
    return pos, pad_pos.astype(jnp.int32), tile_expert, n_used


def _dispatch_kernel(pos_ref, pad_ref, f_ref, xs_ref, sem, *, n_tok):
    i = pl.program_id(0)

    def wait_tile():
        pltpu.make_async_copy(f_ref, xs_ref.at[pl.ds(0, TM)], sem).wait()

    def scatter(k, carry):
        tok = i * TM + k
        pltpu.make_async_copy(f_ref.at[k], xs_ref.at[pos_ref[tok]], sem).start()
        pltpu.make_async_copy(f_ref.at[k], xs_ref.at[pos_ref[n_tok + tok]], sem).start()
        return carry

    lax.fori_loop(0, TM, scatter, 0, unroll=8)

    @pl.when(i == 0)
    def _():
        def fill(k, carry):
            pltpu.make_async_copy(f_ref.at[0], xs_ref.at[pad_ref[k]], sem).start()
            return carry
        lax.fori_loop(0, MOE_PAD, fill, 0, unroll=8)
        for _ in range(MOE_PAD // TM):
            wait_tile()

    wait_tile()
    wait_tile()


def _dispatch_call(pos, pad_pos, f, row0):
    n_tok = N_ROWS - row0
    off = row0 // TM
    return pl.pallas_call(
        functools.partial(_dispatch_kernel, n_tok=n_tok),
        grid_spec=pltpu.PrefetchScalarGridSpec(
            num_scalar_prefetch=2,
            grid=(n_tok // TM,),
            in_specs=[pl.BlockSpec((TM,) + TOKEN_TILE, lambda i, pos, pad: (i + off, 0, 0))],
            out_specs=pl.BlockSpec(memory_space=pl.ANY),
            scratch_shapes=[pltpu.SemaphoreType.DMA(())],
        ),
        out_shape=jax.ShapeDtypeStruct((2 * n_tok + MOE_PAD,) + TOKEN_TILE, F32),
        compiler_params=_params(("arbitrary",)),
        name="moe_dispatch",
    )(pos, pad_pos, f)


def _moe_group_kernel(te_ref, nu_ref, x_ref, wg_ref, wu_ref, wd_ref, o_ref):
    p = pl.program_id(0)
    j = pl.program_id(1)

    @pl.when(p < nu_ref[0])
    def _():
        x = x_ref[...].reshape(MOE_TM, D_MODEL).astype(BF16)
        g = _dot(x, wg_ref[0, 0])
        a = (g * jax.nn.sigmoid(g)) * _dot(x, wu_ref[0, 0])
        part = _dot(a.astype(BF16), wd_ref[0, 0]).reshape(o_ref.shape)

        @pl.when(j == 0)
        def _():
            o_ref[...] = part

        @pl.when(j > 0)
        def _():
            o_ref[...] += part

    @pl.when(p >= nu_ref[0])
    def _():
        o_ref[...] = jnp.zeros_like(o_ref)


def _moe_group_call(tile_expert, n_used, xs, wg, wu, wd, layer):
    n_tiles = xs.shape[0] // MOE_TM
    nf = D_FF // FF_CHUNK
    chunk = lambda p, j, nu: jnp.where(p < nu[0], j, nf - 1)
    return pl.pallas_call(
        _moe_group_kernel,
        grid_spec=pltpu.PrefetchScalarGridSpec(
            num_scalar_prefetch=2,
            grid=(n_tiles, nf),
            in_specs=[
                pl.BlockSpec((MOE_TM,) + TOKEN_TILE, lambda p, j, te, nu: (p, 0, 0)),
                pl.BlockSpec((1, 1, D_MODEL, FF_CHUNK),
                             lambda p, j, te, nu: (layer, te[p], 0, chunk(p, j, nu))),
                pl.BlockSpec((1, 1, D_MODEL, FF_CHUNK),
                             lambda p, j, te, nu: (layer, te[p], 0, chunk(p, j, nu))),
                pl.BlockSpec((1, 1, FF_CHUNK, D_MODEL),
                             lambda p, j, te, nu: (layer, te[p], chunk(p, j, nu), 0)),
            ],
            out_specs=pl.BlockSpec((MOE_TM,) + TOKEN_TILE, lambda p, j, te, nu: (p, 0, 0)),
        ),
        out_shape=jax.ShapeDtypeStruct(xs.shape, F32),
        compiler_params=_params(("arbitrary", "arbitrary")),
        name="moe_group",
    )(tile_expert, n_used, xs, wg, wu, wd)


def _moe_combine_kernel(to_output, n_tok, pos_ref, ys_ref, route_ref, h_ref, mod_ref, gpost_ref, o_ref,
                        ybuf, sem):
    i = pl.program_id(0)

    def request(tile):
        b = tile % 2

        def gather(k, carry):
            tok = tile * TM + k
            pltpu.make_async_copy(ys_ref.at[pos_ref[tok]], ybuf.at[b, 0, k], sem.at[b]).start()
            pltpu.make_async_copy(ys_ref.at[pos_ref[n_tok + tok]], ybuf.at[b, 1, k], sem.at[b]).start()
            return carry

        lax.fori_loop(0, TM, gather, 0, unroll=8)

    @pl.when(i == 0)
    def _():
        request(i)

    @pl.when(i + 1 < pl.num_programs(0))
    def _():
        request(i + 1)

    cur = i % 2
    for slot in range(2):
        pltpu.make_async_copy(ys_ref.at[pl.ds(0, TM)], ybuf.at[cur, slot], sem.at[cur]).wait()

    route = route_ref[...]
    lane = lax.broadcasted_iota(jnp.int32, route.shape, 1)
    w1 = jnp.sum(jnp.where(lane == ROUTE_W1, route, 0.0), axis=-1, keepdims=True)
    w2 = jnp.sum(jnp.where(lane == ROUTE_W2, route, 0.0), axis=-1, keepdims=True)
    y = w1 * ybuf[cur, 0].reshape(TM, D_MODEL) + w2 * ybuf[cur, 1].reshape(TM, D_MODEL)
    hn = _tile_gate_add(h_ref[...], mod_ref[0, 5], _rms(y, gpost_ref[...]))
    o_ref[...] = _from_stream(hn) if to_output else hn


def _moe_combine_call(pos, ys, route, hs, mod, g_post, row0, to_output):
    t = N_ROWS - row0
    nt = t // TM
    off = row0 // TM
    if to_output:
        out_spec = pl.BlockSpec((BATCH, TM // BATCH, D_MODEL), lambda i, pos: (0, i, 0))
        out_shape = jax.ShapeDtypeStruct((BATCH, t // BATCH, D_MODEL), F32)
    else:
        out_spec = pl.BlockSpec((TM, D_MODEL), lambda i, pos: (i, 0))
        out_shape = jax.ShapeDtypeStruct((t, D_MODEL), F32)
    return pl.pallas_call(
        functools.partial(_moe_combine_kernel, to_output, t),
        grid_spec=pltpu.PrefetchScalarGridSpec(
            num_scalar_prefetch=1,
            grid=(nt,),
            in_specs=[
                pl.BlockSpec(memory_space=pl.ANY),
                pl.BlockSpec((TM, LANES), lambda i, pos: (i + off, 0)),
                pl.BlockSpec((TM, D_MODEL), lambda i, pos: (i + off, 0)),
                pl.BlockSpec((1, N_MOD, SUBLANES, D_MODEL), lambda i, pos: (_seg_index(i, off), 0, 0, 0)),
                pl.BlockSpec((1, D_MODEL), lambda i, pos: (0, 0)),
            ],
            out_specs=out_spec,
            scratch_shapes=[pltpu.VMEM((2, 2, TM) + TOKEN_TILE, F32), pltpu.SemaphoreType.DMA((2,))],
        ),
        out_shape=out_shape,
        compiler_params=_params(("arbitrary",)),
        name="moe_combine",
    )(pos, ys, route, hs, mod, g_post)


def _moe_call(f, hs, route, mod, wg, wu, wd, g_post, row0, layer):
    pos, pad_pos, tile_expert, n_used = _dispatch_plan(route, row0)
    xs = _dispatch_call(pos, pad_pos, f, row0)
    ys = _moe_group_call(tile_expert, n_used, xs, wg, wu, wd, layer)
    return _moe_combine_call(pos, ys, route, hs, mod, g_post, row0, to_output=row0 > 0)


def kernel(x, c, ctx, c_ctx, w_ada, b_ada, g_pre_mix, g_post_mix, g_pre_ffn, g_post_ffn, w_in,
           ssm_lam_re, ssm_lam_im, ssm_log_dt, ssm_b_re, ssm_b_im, ssm_c_re, ssm_c_im, ssm_d,
           ssm_w_glu, ssm_b_glu, conv_w, conv_b, conv_ln_g, conv_ln_b, w_out,
           ffn_w_gate, ffn_w_up, ffn_w_down, moe_w_router, moe_b_router, moe_w_gate, moe_w_up, moe_w_down):
    cin = jnp.concatenate([jnp.broadcast_to(c_ctx[None, :], (BATCH, D_MODEL)), c], axis=0)
    mod_all = _ada_call(cin, w_ada, b_ada)
    mod_all = mod_all.reshape(DEPTH, 2, BATCH, N_MOD, D_MODEL).transpose(0, 1, 3, 2, 4)

    w_in_b, w_glu_b, w_out_b = w_in.astype(BF16), ssm_w_glu.astype(BF16), w_out.astype(BF16)
    ffn_b = (ffn_w_gate.astype(BF16), ffn_w_up.astype(BF16), ffn_w_down.astype(BF16))
    moe_b = (moe_w_gate.astype(BF16), moe_w_up.astype(BF16), moe_w_down.astype(BF16))
    s5 = jax.vmap(_s5_params)(ssm_lam_re, ssm_lam_im, ssm_log_dt, ssm_b_re, ssm_b_im, ssm_c_re, ssm_c_im)

    row_vec = lambda v: v.reshape(1, -1).astype(F32)
    for l in range(DEPTH):
        mod = mod_all[l]
        if l == 0:
            u, agl, hs = _mix_in_call((ctx, x), mod, row_vec(g_pre_mix[l]), w_in_b, l)
        else:
            u, agl = _mix_in_call(hs, mod, row_vec(g_pre_mix[l]), w_in_b, l)
        yf, yb = _scan_call(u, l, *s5)
        p = dict(
            ssm_d=row_vec(ssm_d[l]), w_glu=w_glu_b, b_glu=row_vec(ssm_b_glu[l]),
            conv_w=jnp.broadcast_to(conv_w[l][:, None, :], (CONV_WIDTH, SUBLANES, D_CONV)).astype(F32),
            conv_b=row_vec(conv_b[l]), ln_g=row_vec(conv_ln_g[l]), ln_b=row_vec(conv_ln_b[l]),
            w_out=w_out_b, g_post_mix=row_vec(g_post_mix[l]), g_pre_ffn=row_vec(g_pre_ffn[l]))
        i = l // 2
        if l % 2 == 0:
            hs, f = _mix_out_call(hs, u, yf, yb, agl, mod, p, l)
            hs = _ffn_call(f, hs, mod, *ffn_b, row_vec(g_post_ffn[l]), i)
        else:
            wr = jnp.zeros((D_MODEL, LANES), F32).at[:, :N_EXPERTS].set(moe_w_router[i].astype(F32))
            wr_hi = wr.astype(BF16)
            wr_lo = (wr - wr_hi.astype(F32)).astype(BF16)
            br = jnp.zeros((1, LANES), F32).at[0, :N_EXPERTS].set(moe_b_router[i].astype(F32))
            hs, f, route = _mix_out_call(hs, u, yf, yb, agl, mod, p, l, router=(wr_hi, wr_lo, br))
            row0 = 0 if l < DEPTH - 1 else N_CTX
            hs = _moe_call(f, hs, route, mod, *moe_b, row_vec(g_post_ffn[l]), row0, i)
    return hs
```

```python
import functools
import math

import jax
import jax.numpy as jnp
from jax import lax
from jax.experimental import pallas as pl
from jax.experimental.pallas import tpu as pltpu

F32 = jnp.float32
BF16 = jnp.bfloat16

D_MODEL = 1024
BATCH = 8
SEQ = 2048
CTX_LEN = 256
DEPTH = 4
D_SSM = 512
SSM_GROUP = 16
N_SSM_GROUPS = 32
SSM_STATE = 64
D_CONV = 512
CONV_WIDTH = 31
CONV_PAD = CONV_WIDTH // 2
D_IN = D_SSM + 2 * D_CONV
D_FF = 2816
N_EXPERTS = 8
N_MOD = 6
EPS = 1e-6

N_CTX = CTX_LEN * BATCH
N_LAT = SEQ * BATCH
N_ROWS = N_CTX + N_LAT
N_STATE = N_SSM_GROUPS * SSM_STATE

SUBLANES = 8
LANES = 128
VMEM_LIMIT = 56 * 1024 * 1024

TM = 512
SCAN_T = 64
SCAN_R = SCAN_T * BATCH
SCAN_SLABS = 2
SLAB_CH = D_SSM // SCAN_SLABS
SLAB_ST = N_STATE // SCAN_SLABS
B_SLABS = 2
B_SLAB_CH = D_SSM // B_SLABS
B_SLAB_ST = N_STATE // B_SLABS
SCAN_W = 512
HALO = 128
CONV_SUB = 32
FF_CHUNK = 1408
MOE_TM = 512
MOE_PAD = N_EXPERTS * MOE_TM
DMA_UNROLL = 8
ROUTE_E1, ROUTE_E2, ROUTE_W1, ROUTE_W2 = 0, 1, 2, 3
TOKEN_TILE = (SUBLANES, D_MODEL // SUBLANES)


def _dot(a, b):
    return jnp.dot(a, b, preferred_element_type=F32)


def _rms(x, g):
    ms = jnp.mean(x * x, axis=-1, keepdims=True)
    return x * lax.rsqrt(ms + EPS) * g


def _tile_mul_add(y, scale, shift):
    rows, d = y.shape
    y3 = y.reshape(rows // SUBLANES, SUBLANES, d)
    return (y3 * (1.0 + scale)[None] + shift[None]).reshape(rows, d)


def _tile_gate_add(h, gate, r):
    rows, d = h.shape
    r3 = r.reshape(rows // SUBLANES, SUBLANES, d)
    return h + (gate[None] * r3).reshape(rows, d)


def _params(sem):
    return pltpu.CompilerParams(dimension_semantics=sem, vmem_limit_bytes=VMEM_LIMIT)


def _const_spec(shape):
    nd = len(shape)
    return pl.BlockSpec(shape, lambda *_: (0,) * nd)


def _ada_kernel(c_ref, w_ref, b_ref, o_ref):
    c = c_ref[...]
    s = c * jax.nn.sigmoid(c)
    o_ref[0] = _dot(s, w_ref[0]) + b_ref[0]


def _ada_call(cin, w_ada, b_ada):
    tn = 1536
    return pl.pallas_call(
        _ada_kernel,
        grid=(DEPTH, N_MOD * D_MODEL // tn),
        in_specs=[
            pl.BlockSpec((2 * BATCH, D_MODEL), lambda l, j: (0, 0)),
            pl.BlockSpec((1, D_MODEL, tn), lambda l, j: (l, 0, j)),
            pl.BlockSpec((1, 1, tn), lambda l, j: (l, 0, j)),
        ],
        out_specs=pl.BlockSpec((1, 2 * BATCH, tn), lambda l, j: (l, 0, j)),
        out_shape=jax.ShapeDtypeStruct((DEPTH, 2 * BATCH, N_MOD * D_MODEL), F32),
        compiler_params=_params(("arbitrary", "arbitrary")),
        name="ada_mod",
    )(cin, w_ada, b_ada.reshape(DEPTH, 1, N_MOD * D_MODEL))


def _to_stream(x):
    return jnp.swapaxes(x, 0, 1).reshape(x.shape[1] * BATCH, x.shape[2])


def _from_stream(h):
    return jnp.swapaxes(h.reshape(h.shape[0] // BATCH, BATCH, h.shape[1]), 0, 1)


def _mix_in_kernel(from_inputs, *refs):
    if from_inputs:
        ctx_ref, x_ref, mod_ref, g_ref, w_ref, u_ref, a_ref, hs_ref = refs
        is_ctx = pl.program_id(0) < N_CTX // TM

        @pl.when(is_ctx)
        def _():
            hs_ref[...] = _to_stream(ctx_ref[...])

        @pl.when(jnp.logical_not(is_ctx))
        def _():
            hs_ref[...] = _to_stream(x_ref[...])

        h = hs_ref[...]
    else:
        h_ref, mod_ref, g_ref, w_ref, u_ref, a_ref = refs
        h = h_ref[...]
    y = _rms(h, g_ref[...])
    a = _tile_mul_add(y, mod_ref[0, 1], mod_ref[0, 0]).astype(BF16)
    p = _dot(a, w_ref[0])
    u_ref[...] = p[:, :D_SSM]
    v = p[:, D_SSM:D_SSM + D_CONV]
    g = p[:, D_SSM + D_CONV:]
    a_ref[...] = v * jax.nn.sigmoid(g)


def _seg_index(i, off):
    return ((i + off) >= (N_CTX // TM)).astype(jnp.int32)


def _layer_spec(shape, layer, **kwargs):
    return pl.BlockSpec((1,) + shape, lambda *_: (layer,) + (0,) * len(shape), **kwargs)


def _mix_in_call(hs, mod, g_pre, w_in, layer):
    nt = N_ROWS // TM
    from_inputs = isinstance(hs, tuple)
    row = lambda i: (i, 0)
    if from_inputs:
        steps = TM // BATCH
        nct = N_CTX // TM
        h_specs = [pl.BlockSpec((BATCH, steps, D_MODEL), lambda i: (0, jnp.minimum(i, nct - 1), 0)),
                   pl.BlockSpec((BATCH, steps, D_MODEL), lambda i: (0, jnp.maximum(i - nct, 0), 0))]
        h_args = list(hs)
    else:
        h_specs = [pl.BlockSpec((TM, D_MODEL), row)]
        h_args = [hs]
    out_specs = [pl.BlockSpec((TM, D_SSM), row), pl.BlockSpec((TM, D_CONV), row)]
    out_shape = [jax.ShapeDtypeStruct((N_ROWS, D_SSM), F32), jax.ShapeDtypeStruct((N_ROWS, D_CONV), F32)]
    if from_inputs:
        out_specs.append(pl.BlockSpec((TM, D_MODEL), row))
        out_shape.append(jax.ShapeDtypeStruct((N_ROWS, D_MODEL), F32))
    return pl.pallas_call(
        functools.partial(_mix_in_kernel, from_inputs),
        grid=(nt,),
        in_specs=h_specs + [
            pl.BlockSpec((1, N_MOD, SUBLANES, D_MODEL), lambda i: (_seg_index(i, 0), 0, 0, 0)),
            _const_spec((1, D_MODEL)),
            _layer_spec((D_MODEL, D_IN), layer),
        ],
        out_specs=out_specs,
        out_shape=out_shape,
        compiler_params=_params(("arbitrary",)),
        name="mix_in",
    )(*h_args, mod, g_pre, w_in)


def _scan_kernel(uf_ref, ub_ref, bre_ref, bim_ref, cre_ref, cim_ref, lr_ref, li_ref,
                 yf_ref, yb_ref, hre, him, st_re, st_im):
    @pl.when(pl.program_id(0) == 0)
    def _():
        st_re[...] = jnp.zeros_like(st_re)
        st_im[...] = jnp.zeros_like(st_im)

    for d, u_ref in enumerate((uf_ref, ub_ref)):
        u = u_ref[...].astype(BF16)
        for s in range(B_SLABS):
            us = u[:, s * B_SLAB_CH:(s + 1) * B_SLAB_CH]
            hre[d, :, s * B_SLAB_ST:(s + 1) * B_SLAB_ST] = _dot(us, bre_ref[0, d, s])
            him[d, :, s * B_SLAB_ST:(s + 1) * B_SLAB_ST] = _dot(us, bim_ref[0, d, s])

    for d in range(2):
        for c in range(N_STATE // SCAN_W):
            cols = slice(c * SCAN_W, (c + 1) * SCAN_W)
            lam_r = lr_ref[0, d, :, cols]
            lam_i = li_ref[0, d, :, cols]

            def step(t, carry, d=d, cols=cols, lam_r=lam_r, lam_i=lam_i):
                sr, si = carry
                tt = t if d == 0 else SCAN_T - 1 - t
                rows = pl.ds(pl.multiple_of(tt * SUBLANES, SUBLANES), SUBLANES)
                nr = lam_r * sr - lam_i * si + hre[d, rows, cols]
                ni = lam_r * si + lam_i * sr + him[d, rows, cols]
                hre[d, rows, cols] = nr
                him[d, rows, cols] = ni
                return nr, ni

            sr, si = lax.fori_loop(0, SCAN_T, step, (st_re[d, :, cols], st_im[d, :, cols]), unroll=8)
            st_re[d, :, cols] = sr
            st_im[d, :, cols] = si

    for d, y_ref in enumerate((yf_ref, yb_ref)):
        for s in range(SCAN_SLABS):
            st = slice(s * SLAB_ST, (s + 1) * SLAB_ST)
            y_ref[:, s * SLAB_CH:(s + 1) * SLAB_CH] = (
                _dot(hre[d, :, st].astype(BF16), cre_ref[0, d, s])
                + _dot(him[d, :, st].astype(BF16), cim_ref[0, d, s]))


def _bwd_chunk(k):
    nc = N_CTX // SCAN_R
    n = N_ROWS // SCAN_R
    return jnp.where(k < nc, nc - 1 - k, n - 1 + nc - k)


def _scan_call(u, layer, bre, bim, cre, cim, lam_r, lam_i):
    n = N_ROWS // SCAN_R
    return pl.pallas_call(
        _scan_kernel,
        grid=(n,),
        in_specs=[
            pl.BlockSpec((SCAN_R, D_SSM), lambda k: (k, 0)),
            pl.BlockSpec((SCAN_R, D_SSM), lambda k: (_bwd_chunk(k), 0)),
            _layer_spec((2, B_SLABS, B_SLAB_CH, B_SLAB_ST), layer),
            _layer_spec((2, B_SLABS, B_SLAB_CH, B_SLAB_ST), layer),
            _layer_spec((2, SCAN_SLABS, SLAB_ST, SLAB_CH), layer),
            _layer_spec((2, SCAN_SLABS, SLAB_ST, SLAB_CH), layer),
            _layer_spec((2, SUBLANES, N_STATE), layer),
            _layer_spec((2, SUBLANES, N_STATE), layer),
        ],
        out_specs=[
            pl.BlockSpec((SCAN_R, D_SSM), lambda k: (k, 0)),
            pl.BlockSpec((SCAN_R, D_SSM), lambda k: (_bwd_chunk(k), 0)),
        ],
        out_shape=[
            jax.ShapeDtypeStruct((N_ROWS, D_SSM), F32),
            jax.ShapeDtypeStruct((N_ROWS, D_SSM), F32),
        ],
        scratch_shapes=[
            pltpu.VMEM((2, SCAN_R, N_STATE), F32),
            pltpu.VMEM((2, SCAN_R, N_STATE), F32),
            pltpu.VMEM((2, SUBLANES, N_STATE), F32),
            pltpu.VMEM((2, SUBLANES, N_STATE), F32),
        ],
        compiler_params=_params(("arbitrary",)),
        name="s5_scan",
    )(u, u, bre, bim, cre, cim, lam_r, lam_i)


def _s5_params(lam_re, lam_im, log_dt, b_re, b_im, c_re, c_im):
    lr, li = lam_re.astype(F32), lam_im.astype(F32)
    dt = jnp.exp(log_dt.astype(F32))[..., None]
    mag = jnp.exp(lr * dt)
    lbr, lbi = mag * jnp.cos(li * dt), mag * jnp.sin(li * dt)
    den = lr * lr + li * li
    qr = ((lbr - 1.0) * lr + lbi * li) / den
    qi = (lbi * lr - (lbr - 1.0) * li) / den
    br, bi = b_re.astype(F32), b_im.astype(F32)
    bbr = qr[..., None] * br - qi[..., None] * bi
    bbi = qr[..., None] * bi + qi[..., None] * br
    def b_blocks(m):
        gs = N_SSM_GROUPS // B_SLABS
        m = m.reshape(2, B_SLABS, gs, SSM_STATE, SSM_GROUP)
        out = jnp.einsum('dsgph,gk->dsghkp', m, jnp.eye(gs, dtype=F32))
        return out.reshape(2, B_SLABS, B_SLAB_CH, B_SLAB_ST).astype(BF16)

    def c_blocks(m):
        gs = N_SSM_GROUPS // SCAN_SLABS
        m = m.reshape(2, SCAN_SLABS, gs, SSM_GROUP, SSM_STATE)
        out = jnp.einsum('dsghp,gk->dsgpkh', m, jnp.eye(gs, dtype=F32))
        return out.reshape(2, SCAN_SLABS, SLAB_ST, SLAB_CH).astype(BF16)

    def lanes(v):
        return jnp.broadcast_to(v.reshape(2, 1, N_STATE), (2, SUBLANES, N_STATE))

    return (b_blocks(bbr), b_blocks(bbi), c_blocks(c_re.astype(F32)), c_blocks(-c_im.astype(F32)),
            lanes(lbr), lanes(lbi))


def _mix_out_kernel(with_router, h_ref, u_ref, yf_ref, yb_ref, a_ref, ap_ref, an_ref, mod_ref,
                    dsk_ref, wglu_ref, bglu_ref, cw_ref, cb_ref, lng_ref, lnb_ref, wout_ref,
                    gpost_ref, gffn_ref, *rest):
    if with_router:
        wrh_ref, wrl_ref, br_ref, hn_ref, f_ref, route_ref, pad, cvs = rest
    else:
        hn_ref, f_ref, pad, cvs = rest
    i = pl.program_id(0)
    nct = N_CTX // TM
    nt = N_ROWS // TM
    first = jnp.logical_or(i == 0, i == nct)
    last = jnp.logical_or(i == nct - 1, i == nt - 1)
    edge = CONV_PAD * BATCH

    y = yf_ref[...] + yb_ref[...] + dsk_ref[...] * u_ref[...]
    z = jax.nn.gelu(y)
    s = z * jax.nn.sigmoid(_dot(z.astype(BF16), wglu_ref[0]) + bglu_ref[...])

    pad[0:edge, :] = jnp.where(first, 0.0, ap_ref[HALO - edge:HALO, :])
    pad[edge:edge + TM, :] = a_ref[...]
    pad[edge + TM:2 * edge + TM, :] = jnp.where(last, 0.0, an_ref[0:edge, :])

    def conv_block(j, carry):
        r0 = pl.multiple_of(j * CONV_SUB, CONV_SUB)
        ng = CONV_SUB // BATCH
        step = lambda s: pad[pl.ds(r0 + s * BATCH, BATCH), :]
        win = [step(s) for s in range(ng)]
        acc = [jnp.zeros((BATCH, D_CONV), F32) for _ in range(ng)]
        for k in range(CONV_WIDTH):
            w = cw_ref[k]
            acc = [acc[q] + win[q] * w for q in range(ng)]
            if k + 1 < CONV_WIDTH:
                win = win[1:] + [step(k + ng)]
        for q in range(ng):
            cvs[pl.ds(r0 + q * BATCH, BATCH), :] = acc[q]
        return carry

    lax.fori_loop(0, TM // CONV_SUB, conv_block, 0)
    cv = cvs[...] + cb_ref[...]
    mu = jnp.mean(cv, axis=-1, keepdims=True)
    xc = cv - mu
    var = jnp.mean(xc * xc, axis=-1, keepdims=True)
    cv = xc * lax.rsqrt(var + EPS) * lng_ref[...] + lnb_ref[...]
    cv = cv * jax.nn.sigmoid(cv)

    out = _dot(s.astype(BF16), wout_ref[0, 0:D_SSM, :]) + _dot(cv.astype(BF16), wout_ref[0, D_SSM:, :])
    hn = _tile_gate_add(h_ref[...], mod_ref[0, 2], _rms(out, gpost_ref[...]))
    hn_ref[...] = hn
    f = _tile_mul_add(_rms(hn, gffn_ref[...]), mod_ref[0, 4], mod_ref[0, 3])
    if with_router:
        f_ref[...] = f.reshape(f_ref.shape)
    else:
        f_ref[...] = f.astype(BF16)

    if with_router:
        f_hi = f.astype(BF16)
        f_lo = (f - f_hi.astype(F32)).astype(BF16)
        logits = (_dot(f_hi, wrh_ref[...]) + _dot(f_lo, wrh_ref[...]) + _dot(f_hi, wrl_ref[...])
                  + br_ref[...])
        lane = lax.broadcasted_iota(jnp.int32, logits.shape, 1)
        neg = jnp.float32(-jnp.inf)
        logits = jnp.where(lane < N_EXPERTS, logits, neg)
        m1 = jnp.max(logits, axis=-1, keepdims=True)
        i1 = jnp.min(jnp.where(logits == m1, lane, LANES), axis=-1, keepdims=True)
        rest_l = jnp.where(lane == i1, neg, logits)
        m2 = jnp.max(rest_l, axis=-1, keepdims=True)
        i2 = jnp.min(jnp.where(rest_l == m2, lane, LANES), axis=-1, keepdims=True)
        e2 = jnp.exp(m2 - m1)
        den = 1.0 + e2
        route = jnp.where(lane == ROUTE_E1, i1.astype(F32), 0.0)
        route = jnp.where(lane == ROUTE_E2, i2.astype(F32), route)
        route = jnp.where(lane == ROUTE_W1, 1.0 / den, route)
        route_ref[...] = jnp.where(lane == ROUTE_W2, e2 / den, route)


def _mix_out_call(hs, u, yf, yb, agl, mod, p, layer, router=None):
    nt = N_ROWS // TM
    hb = TM // HALO
    nh = N_ROWS // HALO
    row = lambda i: (i, 0)
    in_specs = [
        pl.BlockSpec((TM, D_MODEL), row),
        pl.BlockSpec((TM, D_SSM), row),
        pl.BlockSpec((TM, D_SSM), row),
        pl.BlockSpec((TM, D_SSM), row),
        pl.BlockSpec((TM, D_CONV), row),
        pl.BlockSpec((HALO, D_CONV), lambda i: (jnp.maximum(i * hb - 1, 0), 0)),
        pl.BlockSpec((HALO, D_CONV), lambda i: (jnp.minimum((i + 1) * hb, nh - 1), 0)),
        pl.BlockSpec((1, N_MOD, SUBLANES, D_MODEL), lambda i: (_seg_index(i, 0), 0, 0, 0)),
        _const_spec((1, D_SSM)),
        _layer_spec((D_SSM, D_SSM), layer),
        _const_spec((1, D_SSM)),
        _const_spec((CONV_WIDTH, SUBLANES, D_CONV)),
        _const_spec((1, D_CONV)),
        _const_spec((1, D_CONV)),
        _const_spec((1, D_CONV)),
        _layer_spec((D_MODEL, D_MODEL), layer),
        _const_spec((1, D_MODEL)),
        _const_spec((1, D_MODEL)),
    ]
    args = [hs, u, yf, yb, agl, agl, agl, mod, p["ssm_d"], p["w_glu"], p["b_glu"], p["conv_w"],
            p["conv_b"], p["ln_g"], p["ln_b"], p["w_out"], p["g_post_mix"], p["g_pre_ffn"]]
    out_specs = [pl.BlockSpec((TM, D_MODEL), row)]
    out_shape = [jax.ShapeDtypeStruct((N_ROWS, D_MODEL), F32)]
    if router is None:
        out_specs.append(pl.BlockSpec((TM, D_MODEL), row))
        out_shape.append(jax.ShapeDtypeStruct((N_ROWS, D_MODEL), BF16))
    else:
        in_specs += [_const_spec((D_MODEL, LANES)), _const_spec((D_MODEL, LANES)), _const_spec((1, LANES))]
        args += list(router)
        out_specs += [pl.BlockSpec((TM,) + TOKEN_TILE, lambda i: (i, 0, 0)), pl.BlockSpec((TM, LANES), row)]
        out_shape += [jax.ShapeDtypeStruct((N_ROWS,) + TOKEN_TILE, F32),
                      jax.ShapeDtypeStruct((N_ROWS, LANES), F32)]
    return pl.pallas_call(
        functools.partial(_mix_out_kernel, router is not None),
        grid=(nt,),
        in_specs=in_specs,
        out_specs=out_specs,
        out_shape=out_shape,
        scratch_shapes=[
            pltpu.VMEM((TM + 2 * CONV_PAD * BATCH, D_CONV), F32),
            pltpu.VMEM((TM, D_CONV), F32),
        ],
        compiler_params=_params(("arbitrary",)),
        name="mix_out",
    )(*args)


def _ffn_kernel(f_ref, h_ref, mod_ref, wg_ref, wu_ref, wd_ref, gpost_ref, o_ref):
    f = f_ref[...]
    acc = None
    for j in range(D_FF // FF_CHUNK):
        cols = slice(j * FF_CHUNK, (j + 1) * FF_CHUNK)
        g = _dot(f, wg_ref[0, :, cols])
        a = (g * jax.nn.sigmoid(g)) * _dot(f, wu_ref[0, :, cols])
        part = _dot(a.astype(BF16), wd_ref[0, cols, :])
        acc = part if acc is None else acc + part
    o_ref[...] = _tile_gate_add(h_ref[...], mod_ref[0, 5], _rms(acc, gpost_ref[...]))


def _ffn_call(f, hs, mod, wg, wu, wd, g_post, layer):
    nt = N_ROWS // TM
    row = lambda i: (i, 0)
    once = dict(pipeline_mode=pl.Buffered(1))
    return pl.pallas_call(
        _ffn_kernel,
        grid=(nt,),
        in_specs=[
            pl.BlockSpec((TM, D_MODEL), row),
            pl.BlockSpec((TM, D_MODEL), row),
            pl.BlockSpec((1, N_MOD, SUBLANES, D_MODEL), lambda i: (_seg_index(i, 0), 0, 0, 0)),
            _layer_spec((D_MODEL, D_FF), layer, **once),
            _layer_spec((D_MODEL, D_FF), layer, **once),
            _layer_spec((D_FF, D_MODEL), layer, **once),
            _const_spec((1, D_MODEL)),
        ],
        out_specs=pl.BlockSpec((TM, D_MODEL), row),
        out_shape=jax.ShapeDtypeStruct((N_ROWS, D_MODEL), F32),
        compiler_params=_params(("arbitrary",)),
        name="ffn_dense",
    )(f, hs, mod, wg, wu, wd, g_post)


def _dispatch_plan(route, row0):
    r = route[row0:]
    e = jnp.concatenate([r[:, ROUTE_E1], r[:, ROUTE_E2]]).astype(jnp.int32)
    ids = jnp.arange(N_EXPERTS, dtype=jnp.int32)
    onehot = (e[:, None] == ids[None, :]).astype(jnp.int32)
    csum = jnp.cumsum(onehot, axis=0)
    counts = csum[-1]
    rank = jnp.sum((csum - onehot) * onehot, axis=1)
    padded = ((counts + MOE_TM - 1) // MOE_TM) * MOE_TM
    ends = jnp.cumsum(padded)
    offs = ends - padded
    pos = jnp.sum(onehot * offs[None, :], axis=1) + rank

    padlen = padded - counts
    cp_end = jnp.cumsum(padlen)
    cp = cp_end - padlen
    j = jnp.arange(MOE_PAD, dtype=jnp.int32)
    ej = jnp.minimum(jnp.sum((j[:, None] >= cp_end[None, :]).astype(jnp.int32), axis=1), N_EXPERTS - 1)
    pad_pos = jnp.where(j < cp_end[-1], offs[ej] + counts[ej] + (j - cp[ej]), ends[-1] + (j - cp_end[-1]))

    n_tiles = (2 * r.shape[0] + MOE_PAD) // MOE_TM
    starts = jnp.arange(n_tiles, dtype=jnp.int32) * MOE_TM
    tile_expert = jnp.minimum(jnp.sum((starts[:, None] >= ends[None, :]).astype(jnp.int32), axis=1),
                              N_EXPERTS - 1)
    n_used = (ends[-1] // MOE_TM).reshape(1)
    return pos, pad_pos.astype(jnp.int32), tile_expert, n_used


def _dispatch_kernel(pos_ref, pad_ref, f_ref, xs_ref, sem, *, n_tok):
    i = pl.program_id(0)

    def wait_tile():
        pltpu.make_async_copy(f_ref, xs_ref.at[pl.ds(0, TM)], sem).wait()

    def scatter(k8, carry):
        for j in range(DMA_UNROLL):
            k = k8 * DMA_UNROLL + j
            tok = i * TM + k
            pltpu.make_async_copy(f_ref.at[k], xs_ref.at[pos_ref[tok]], sem).start(priority=0)
            pltpu.make_async_copy(f_ref.at[k], xs_ref.at[pos_ref[n_tok + tok]], sem).start(priority=1)
        return carry

    lax.fori_loop(0, TM // DMA_UNROLL, scatter, 0)

    @pl.when(i == 0)
    def _():
        def fill(k8, carry):
            for j in range(DMA_UNROLL):
                pltpu.make_async_copy(f_ref.at[0], xs_ref.at[pad_ref[k8 * DMA_UNROLL + j]],
                                      sem).start(priority=j % 2)
            return carry
        lax.fori_loop(0, MOE_PAD // DMA_UNROLL, fill, 0)
        for _ in range(MOE_PAD // TM):
            wait_tile()

    wait_tile()
    wait_tile()


def _dispatch_call(pos, pad_pos, f, row0):
    n_tok = N_ROWS - row0
    off = row0 // TM
    return pl.pallas_call(
        functools.partial(_dispatch_kernel, n_tok=n_tok),
        grid_spec=pltpu.PrefetchScalarGridSpec(
            num_scalar_prefetch=2,
            grid=(n_tok // TM,),
            in_specs=[pl.BlockSpec((TM,) + TOKEN_TILE, lambda i, pos, pad: (i + off, 0, 0))],
            out_specs=pl.BlockSpec(memory_space=pl.ANY),
            scratch_shapes=[pltpu.SemaphoreType.DMA(())],
        ),
        out_shape=jax.ShapeDtypeStruct((2 * n_tok + MOE_PAD,) + TOKEN_TILE, F32),
        compiler_params=_params(("arbitrary",)),
        name="moe_dispatch",
    )(pos, pad_pos, f)


def _moe_group_kernel(te_ref, nu_ref, x_ref, wg_ref, wu_ref, wd_ref, o_ref, xb, acc):
    p = pl.program_id(0)
    j = pl.program_id(1)
    last = pl.num_programs(1) - 1

    @pl.when(p < nu_ref[0])
    def _():
        @pl.when(j == 0)
        def _():
            xb[...] = x_ref[...].reshape(MOE_TM, D_MODEL).astype(BF16)

        x = xb[...]
        g = _dot(x, wg_ref[0, 0])
        a = (g * jax.nn.sigmoid(g)) * _dot(x, wu_ref[0, 0])
        part = _dot(a.astype(BF16), wd_ref[0, 0])

        @pl.when(j == 0)
        def _():
            acc[...] = part

        @pl.when(jnp.logical_and(j > 0, j < last))
        def _():
            acc[...] += part

        @pl.when(j == last)
        def _():
            o_ref[...] = (acc[...] + part).reshape(o_ref.shape)

    @pl.when(p >= nu_ref[0])
    def _():
        o_ref[...] = jnp.zeros_like(o_ref)


def _moe_group_call(tile_expert, n_used, xs, wg, wu, wd, layer):
    n_tiles = xs.shape[0] // MOE_TM
    nf = D_FF // FF_CHUNK
    assert nf >= 2
    chunk = lambda p, j, nu: jnp.where(p < nu[0], j, nf - 1)
    return pl.pallas_call(
        _moe_group_kernel,
        grid_spec=pltpu.PrefetchScalarGridSpec(
            num_scalar_prefetch=2,
            grid=(n_tiles, nf),
            in_specs=[
                pl.BlockSpec((MOE_TM,) + TOKEN_TILE, lambda p, j, te, nu: (p, 0, 0)),
                pl.BlockSpec((1, 1, D_MODEL, FF_CHUNK),
                             lambda p, j, te, nu: (layer, te[p], 0, chunk(p, j, nu))),
                pl.BlockSpec((1, 1, D_MODEL, FF_CHUNK),
                             lambda p, j, te, nu: (layer, te[p], 0, chunk(p, j, nu))),
                pl.BlockSpec((1, 1, FF_CHUNK, D_MODEL),
                             lambda p, j, te, nu: (layer, te[p], chunk(p, j, nu), 0)),
            ],
            out_specs=pl.BlockSpec((MOE_TM,) + TOKEN_TILE, lambda p, j, te, nu: (p, 0, 0)),
            scratch_shapes=[pltpu.VMEM((MOE_TM, D_MODEL), BF16), pltpu.VMEM((MOE_TM, D_MODEL), F32)],
        ),
        out_shape=jax.ShapeDtypeStruct(xs.shape, F32),
        compiler_params=_params(("arbitrary", "arbitrary")),
        name="moe_group",
    )(tile_expert, n_used, xs, wg, wu, wd)


def _moe_combine_kernel(to_output, n_tok, pos_ref, ys_ref, route_ref, h_ref, mod_ref, gpost_ref, o_ref,
                        ybuf, sem):
    i = pl.program_id(0)

    def request(tile):
        b = tile % 2

        def gather(k8, carry):
            for j in range(DMA_UNROLL):
                k = k8 * DMA_UNROLL + j
                tok = tile * TM + k
                pltpu.make_async_copy(ys_ref.at[pos_ref[tok]], ybuf.at[b, 0, k], sem.at[b]).start(priority=0)
                pltpu.make_async_copy(ys_ref.at[pos_ref[n_tok + tok]], ybuf.at[b, 1, k],
                                      sem.at[b]).start(priority=1)
            return carry

        lax.fori_loop(0, TM // DMA_UNROLL, gather, 0)

    @pl.when(i == 0)
    def _():
        request(i)

    @pl.when(i + 1 < pl.num_programs(0))
    def _():
        request(i + 1)

    cur = i % 2
    for slot in range(2):
        pltpu.make_async_copy(ys_ref.at[pl.ds(0, TM)], ybuf.at[cur, slot], sem.at[cur]).wait()

    route = route_ref[...]
    lane = lax.broadcasted_iota(jnp.int32, route.shape, 1)
    w1 = jnp.sum(jnp.where(lane == ROUTE_W1, route, 0.0), axis=-1, keepdims=True)
    w2 = jnp.sum(jnp.where(lane == ROUTE_W2, route, 0.0), axis=-1, keepdims=True)
    y = w1 * ybuf[cur, 0].reshape(TM, D_MODEL) + w2 * ybuf[cur, 1].reshape(TM, D_MODEL)
    hn = _tile_gate_add(h_ref[...], mod_ref[0, 5], _rms(y, gpost_ref[...]))
    o_ref[...] = _from_stream(hn) if to_output else hn


def _moe_combine_call(pos, ys, route, hs, mod, g_post, row0, to_output):
    t = N_ROWS - row0
    nt = t // TM
    off = row0 // TM
    if to_output:
        out_spec = pl.BlockSpec((BATCH, TM // BATCH, D_MODEL), lambda i, pos: (0, i, 0))
        out_shape = jax.ShapeDtypeStruct((BATCH, t // BATCH, D_MODEL), F32)
    else:
        out_spec = pl.BlockSpec((TM, D_MODEL), lambda i, pos: (i, 0))
        out_shape = jax.ShapeDtypeStruct((t, D_MODEL), F32)
    return pl.pallas_call(
        functools.partial(_moe_combine_kernel, to_output, t),
        grid_spec=pltpu.PrefetchScalarGridSpec(
            num_scalar_prefetch=1,
            grid=(nt,),
            in_specs=[
                pl.BlockSpec(memory_space=pl.ANY),
                pl.BlockSpec((TM, LANES), lambda i, pos: (i + off, 0)),
                pl.BlockSpec((TM, D_MODEL), lambda i, pos: (i + off, 0)),
                pl.BlockSpec((1, N_MOD, SUBLANES, D_MODEL), lambda i, pos: (_seg_index(i, off), 0, 0, 0)),
                pl.BlockSpec((1, D_MODEL), lambda i, pos: (0, 0)),
            ],
            out_specs=out_spec,
            scratch_shapes=[pltpu.VMEM((2, 2, TM) + TOKEN_TILE, F32), pltpu.SemaphoreType.DMA((2,))],
        ),
        out_shape=out_shape,
        compiler_params=_params(("arbitrary",)),
        name="moe_combine",
    )(pos, ys, route, hs, mod, g_post)


def _moe_call(f, hs, route, mod, wg, wu, wd, g_post, row0, layer):
    pos, pad_pos, tile_expert, n_used = _dispatch_plan(route, row0)
    xs = _dispatch_call(pos, pad_pos, f, row0)
    ys = _moe_group_call(tile_expert, n_used, xs, wg, wu, wd, layer)
    return _moe_combine_call(pos, ys, route, hs, mod, g_post, row0, to_output=row0 > 0)


def kernel(x, c, ctx, c_ctx, w_ada, b_ada, g_pre_mix, g_post_mix, g_pre_ffn, g_post_ffn, w_in,
           ssm_lam_re, ssm_lam_im, ssm_log_dt, ssm_b_re, ssm_b_im, ssm_c_re, ssm_c_im, ssm_d,
           ssm_w_glu, ssm_b_glu, conv_w, conv_b, conv_ln_g, conv_ln_b, w_out,
           ffn_w_gate, ffn_w_up, ffn_w_down, moe_w_router, moe_b_router, moe_w_gate, moe_w_up, moe_w_down):
    cin = jnp.concatenate([jnp.broadcast_to(c_ctx[None, :], (BATCH, D_MODEL)), c], axis=0)
    mod_all = _ada_call(cin, w_ada, b_ada)
    mod_all = mod_all.reshape(DEPTH, 2, BATCH, N_MOD, D_MODEL).transpose(0, 1, 3, 2, 4)

    w_in_b, w_glu_b, w_out_b = w_in.astype(BF16), ssm_w_glu.astype(BF16), w_out.astype(BF16)
    ffn_b = (ffn_w_gate.astype(BF16), ffn_w_up.astype(BF16), ffn_w_down.astype(BF16))
    moe_b = (moe_w_gate.astype(BF16), moe_w_up.astype(BF16), moe_w_down.astype(BF16))
    s5 = jax.vmap(_s5_params)(ssm_lam_re, ssm_lam_im, ssm_log_dt, ssm_b_re, ssm_b_im, ssm_c_re, ssm_c_im)

    row_vec = lambda v: v.reshape(1, -1).astype(F32)
    for l in range(DEPTH):
        mod = mod_all[l]
        if l == 0:
            u, agl, hs = _mix_in_call((ctx, x), mod, row_vec(g_pre_mix[l]), w_in_b, l)
        else:
            u, agl = _mix_in_call(hs, mod, row_vec(g_pre_mix[l]), w_in_b, l)
        yf, yb = _scan_call(u, l, *s5)
        p = dict(
            ssm_d=row_vec(ssm_d[l]), w_glu=w_glu_b, b_glu=row_vec(ssm_b_glu[l]),
            conv_w=jnp.broadcast_to(conv_w[l][:, None, :], (CONV_WIDTH, SUBLANES, D_CONV)).astype(F32),
            conv_b=row_vec(conv_b[l]), ln_g=row_vec(conv_ln_g[l]), ln_b=row_vec(conv_ln_b[l]),
            w_out=w_out_b, g_post_mix=row_vec(g_post_mix[l]), g_pre_ffn=row_vec(g_pre_ffn[l]))
        i = l // 2
        if l % 2 == 0:
            hs, f = _mix_out_call(hs, u, yf, yb, agl, mod, p, l)
            hs = _ffn_call(f, hs, mod, *ffn_b, row_vec(g_post_ffn[l]), i)
        else:
            wr = jnp.zeros((D_MODEL, LANES), F32).at[:, :N_EXPERTS].set(moe_w_router[i].astype(F32))
            wr_hi = wr.astype(BF16)
            wr_lo = (wr - wr_hi.astype(F32)).astype(BF16)
            br = jnp.zeros((1, LANES), F32).at[0, :N_EXPERTS].set(moe_b_router[i].astype(F32))
            hs, f, route = _mix_out_call(hs, u, yf, yb, agl, mod, p, l, router=(wr_hi, wr_lo, br))
            row0 = 0 if l < DEPTH - 1 else N_CTX
            hs = _moe_call(f, hs, route, mod, *moe_b, row_vec(g_post_ffn[l]), row0, i)
    return hs
```

```python
import functools
import math

import jax
import jax.numpy as jnp
from jax import lax
from jax.experimental import pallas as pl
from jax.experimental.pallas import tpu as pltpu

F32 = jnp.float32
BF16 = jnp.bfloat16

D_MODEL = 1024
BATCH = 8
SEQ = 2048
CTX_LEN = 256
DEPTH = 4
D_SSM = 512
SSM_GROUP = 16
N_SSM_GROUPS = 32
SSM_STATE = 64
D_CONV = 512
CONV_WIDTH = 31
CONV_PAD = CONV_WIDTH // 2
D_IN = D_SSM + 2 * D_CONV
D_FF = 2816
N_EXPERTS = 8
N_MOD = 6
EPS = 1e-6

N_CTX = CTX_LEN * BATCH
N_LAT = SEQ * BATCH
N_ROWS = N_CTX + N_LAT
N_STATE = N_SSM_GROUPS * SSM_STATE

SUBLANES = 8
LANES = 128
VMEM_LIMIT = 56 * 1024 * 1024

TM = 512
SCAN_T = 64
SCAN_R = SCAN_T * BATCH
SCAN_SLABS = 2
SLAB_CH = D_SSM // SCAN_SLABS
SLAB_ST = N_STATE // SCAN_SLABS
B_SLABS = 2
B_SLAB_CH = D_SSM // B_SLABS
B_SLAB_ST = N_STATE // B_SLABS
SCAN_W = 512
HALO = 128
CONV_SUB = 32
FF_CHUNK = 1408
MOE_TM = 512
MOE_PAD = N_EXPERTS * MOE_TM
DMA_UNROLL = 8
ROUTE_E1, ROUTE_E2, ROUTE_W1, ROUTE_W2 = 0, 1, 2, 3
TOKEN_TILE = (SUBLANES, D_MODEL // SUBLANES)


def _dot(a, b):
    return jnp.dot(a, b, preferred_element_type=F32)


def _rms(x, g):
    ms = jnp.mean(x * x, axis=-1, keepdims=True)
    return x * lax.rsqrt(ms + EPS) * g


def _tile_mul_add(y, scale, shift):
    rows, d = y.shape
    y3 = y.reshape(rows // SUBLANES, SUBLANES, d)
    return (y3 * (1.0 + scale)[None] + shift[None]).reshape(rows, d)


def _tile_gate_add(h, gate, r):
    rows, d = h.shape
    r3 = r.reshape(rows // SUBLANES, SUBLANES, d)
    return h + (gate[None] * r3).reshape(rows, d)


def _params(sem):
    return pltpu.CompilerParams(dimension_semantics=sem, vmem_limit_bytes=VMEM_LIMIT)


def _const_spec(shape):
    nd = len(shape)
    return pl.BlockSpec(shape, lambda *_: (0,) * nd)


def _ada_kernel(c_ref, w_ref, b_ref, o_ref):
    c = c_ref[...]
    s = c * jax.nn.sigmoid(c)
    o_ref[0] = _dot(s, w_ref[0]) + b_ref[0]


def _ada_call(cin, w_ada, b_ada):
    tn = 1536
    return pl.pallas_call(
        _ada_kernel,
        grid=(DEPTH, N_MOD * D_MODEL // tn),
        in_specs=[
            pl.BlockSpec((2 * BATCH, D_MODEL), lambda l, j: (0, 0)),
            pl.BlockSpec((1, D_MODEL, tn), lambda l, j: (l, 0, j)),
            pl.BlockSpec((1, 1, tn), lambda l, j: (l, 0, j)),
        ],
        out_specs=pl.BlockSpec((1, 2 * BATCH, tn), lambda l, j: (l, 0, j)),
        out_shape=jax.ShapeDtypeStruct((DEPTH, 2 * BATCH, N_MOD * D_MODEL), F32),
        compiler_params=_params(("arbitrary", "arbitrary")),
        name="ada_mod",
    )(cin, w_ada, b_ada.reshape(DEPTH, 1, N_MOD * D_MODEL))


def _to_stream(x):
    return jnp.swapaxes(x, 0, 1).reshape(x.shape[1] * BATCH, x.shape[2])


def _from_stream(h):
    return jnp.swapaxes(h.reshape(h.shape[0] // BATCH, BATCH, h.shape[1]), 0, 1)


def _mix_in_kernel(from_inputs, *refs):
    if from_inputs:
        ctx_ref, x_ref, mod_ref, g_ref, w_ref, u_ref, a_ref, hs_ref = refs
        is_ctx = pl.program_id(0) < N_CTX // TM

        @pl.when(is_ctx)
        def _():
            hs_ref[...] = _to_stream(ctx_ref[...])

        @pl.when(jnp.logical_not(is_ctx))
        def _():
            hs_ref[...] = _to_stream(x_ref[...])

        h = hs_ref[...]
    else:
        h_ref, mod_ref, g_ref, w_ref, u_ref, a_ref = refs
        h = h_ref[...]
    y = _rms(h, g_ref[...])
    a = _tile_mul_add(y, mod_ref[0, 1], mod_ref[0, 0]).astype(BF16)
    p = _dot(a, w_ref[0])
    u_ref[...] = p[:, :D_SSM]
    v = p[:, D_SSM:D_SSM + D_CONV]
    g = p[:, D_SSM + D_CONV:]
    a_ref[...] = v * jax.nn.sigmoid(g)


def _seg_index(i, off):
    return ((i + off) >= (N_CTX // TM)).astype(jnp.int32)


def _layer_spec(shape, layer, **kwargs):
    return pl.BlockSpec((1,) + shape, lambda *_: (layer,) + (0,) * len(shape), **kwargs)


def _mix_in_call(hs, mod, g_pre, w_in, layer):
    nt = N_ROWS // TM
    from_inputs = isinstance(hs, tuple)
    row = lambda i: (i, 0)
    if from_inputs:
        steps = TM // BATCH
        nct = N_CTX // TM
        h_specs = [pl.BlockSpec((BATCH, steps, D_MODEL), lambda i: (0, jnp.minimum(i, nct - 1), 0)),
                   pl.BlockSpec((BATCH, steps, D_MODEL), lambda i: (0, jnp.maximum(i - nct, 0), 0))]
        h_args = list(hs)
    else:
        h_specs = [pl.BlockSpec((TM, D_MODEL), row)]
        h_args = [hs]
    out_specs = [pl.BlockSpec((TM, D_SSM), row), pl.BlockSpec((TM, D_CONV), row)]
    out_shape = [jax.ShapeDtypeStruct((N_ROWS, D_SSM), F32), jax.ShapeDtypeStruct((N_ROWS, D_CONV), F32)]
    if from_inputs:
        out_specs.append(pl.BlockSpec((TM, D_MODEL), row))
        out_shape.append(jax.ShapeDtypeStruct((N_ROWS, D_MODEL), F32))
    return pl.pallas_call(
        functools.partial(_mix_in_kernel, from_inputs),
        grid=(nt,),
        in_specs=h_specs + [
            pl.BlockSpec((1, N_MOD, SUBLANES, D_MODEL), lambda i: (_seg_index(i, 0), 0, 0, 0)),
            _const_spec((1, D_MODEL)),
            _layer_spec((D_MODEL, D_IN), layer),
        ],
        out_specs=out_specs,
        out_shape=out_shape,
        compiler_params=_params(("arbitrary",)),
        name="mix_in",
    )(*h_args, mod, g_pre, w_in)


def _scan_kernel(uf_ref, ub_ref, bre_ref, bim_ref, cre_ref, cim_ref, lr_ref, li_ref,
                 yf_ref, yb_ref, hre, him, st_re, st_im):
    @pl.when(pl.program_id(0) == 0)
    def _():
        st_re[...] = jnp.zeros_like(st_re)
        st_im[...] = jnp.zeros_like(st_im)

    for d, u_ref in enumerate((uf_ref, ub_ref)):
        u = u_ref[...].astype(BF16)
        for s in range(B_SLABS):
            us = u[:, s * B_SLAB_CH:(s + 1) * B_SLAB_CH]
            hre[d, :, s * B_SLAB_ST:(s + 1) * B_SLAB_ST] = _dot(us, bre_ref[0, d, s])
            him[d, :, s * B_SLAB_ST:(s + 1) * B_SLAB_ST] = _dot(us, bim_ref[0, d, s])

    for d in range(2):
        for c in range(N_STATE // SCAN_W):
            cols = slice(c * SCAN_W, (c + 1) * SCAN_W)
            lam_r = lr_ref[0, d, :, cols]
            lam_i = li_ref[0, d, :, cols]

            def step(t, carry, d=d, cols=cols, lam_r=lam_r, lam_i=lam_i):
                sr, si = carry
                tt = t if d == 0 else SCAN_T - 1 - t
                rows = pl.ds(pl.multiple_of(tt * SUBLANES, SUBLANES), SUBLANES)
                nr = lam_r * sr - lam_i * si + hre[d, rows, cols]
                ni = lam_r * si + lam_i * sr + him[d, rows, cols]
                hre[d, rows, cols] = nr
                him[d, rows, cols] = ni
                return nr, ni

            sr, si = lax.fori_loop(0, SCAN_T, step, (st_re[d, :, cols], st_im[d, :, cols]), unroll=8)
            st_re[d, :, cols] = sr
            st_im[d, :, cols] = si

    for d, y_ref in enumerate((yf_ref, yb_ref)):
        for s in range(SCAN_SLABS):
            st = slice(s * SLAB_ST, (s + 1) * SLAB_ST)
            y_ref[:, s * SLAB_CH:(s + 1) * SLAB_CH] = (
                _dot(hre[d, :, st].astype(BF16), cre_ref[0, d, s])
                + _dot(him[d, :, st].astype(BF16), cim_ref[0, d, s]))


def _bwd_chunk(k):
    nc = N_CTX // SCAN_R
    n = N_ROWS // SCAN_R
    return jnp.where(k < nc, nc - 1 - k, n - 1 + nc - k)


def _scan_call(u, layer, bre, bim, cre, cim, lam_r, lam_i):
    n = N_ROWS // SCAN_R
    return pl.pallas_call(
        _scan_kernel,
        grid=(n,),
        in_specs=[
            pl.BlockSpec((SCAN_R, D_SSM), lambda k: (k, 0)),
            pl.BlockSpec((SCAN_R, D_SSM), lambda k: (_bwd_chunk(k), 0)),
            _layer_spec((2, B_SLABS, B_SLAB_CH, B_SLAB_ST), layer),
            _layer_spec((2, B_SLABS, B_SLAB_CH, B_SLAB_ST), layer),
            _layer_spec((2, SCAN_SLABS, SLAB_ST, SLAB_CH), layer),
            _layer_spec((2, SCAN_SLABS, SLAB_ST, SLAB_CH), layer),
            _layer_spec((2, SUBLANES, N_STATE), layer),
            _layer_spec((2, SUBLANES, N_STATE), layer),
        ],
        out_specs=[
            pl.BlockSpec((SCAN_R, D_SSM), lambda k: (k, 0)),
            pl.BlockSpec((SCAN_R, D_SSM), lambda k: (_bwd_chunk(k), 0)),
        ],
        out_shape=[
            jax.ShapeDtypeStruct((N_ROWS, D_SSM), F32),
            jax.ShapeDtypeStruct((N_ROWS, D_SSM), F32),
        ],
        scratch_shapes=[
            pltpu.VMEM((2, SCAN_R, N_STATE), F32),
            pltpu.VMEM((2, SCAN_R, N_STATE), F32),
            pltpu.VMEM((2, SUBLANES, N_STATE), F32),
            pltpu.VMEM((2, SUBLANES, N_STATE), F32),
        ],
        compiler_params=_params(("arbitrary",)),
        name="s5_scan",
    )(u, u, bre, bim, cre, cim, lam_r, lam_i)


def _s5_params(lam_re, lam_im, log_dt, b_re, b_im, c_re, c_im):
    lr, li = lam_re.astype(F32), lam_im.astype(F32)
    dt = jnp.exp(log_dt.astype(F32))[..., None]
    mag = jnp.exp(lr * dt)
    lbr, lbi = mag * jnp.cos(li * dt), mag * jnp.sin(li * dt)
    den = lr * lr + li * li
    qr = ((lbr - 1.0) * lr + lbi * li) / den
    qi = (lbi * lr - (lbr - 1.0) * li) / den
    br, bi = b_re.astype(F32), b_im.astype(F32)
    bbr = qr[..., None] * br - qi[..., None] * bi
    bbi = qr[..., None] * bi + qi[..., None] * br
    def block_diag(m, slabs):
        gs = N_SSM_GROUPS // slabs
        a, b = m.shape[2:]
        mt = jnp.swapaxes(m.reshape(2, slabs, gs, a, b), 3, 4)
        same = (jnp.arange(gs)[:, None] == jnp.arange(gs)[None, :])[:, None, :, None]
        out = jnp.where(same, mt[:, :, :, :, None, :], 0.0)
        return out.reshape(2, slabs, gs * b, gs * a).astype(BF16)

    def b_blocks(m):
        return block_diag(m, B_SLABS)

    def c_blocks(m):
        return block_diag(m, SCAN_SLABS)

    def lanes(v):
        return jnp.broadcast_to(v.reshape(2, 1, N_STATE), (2, SUBLANES, N_STATE))

    return (b_blocks(bbr), b_blocks(bbi), c_blocks(c_re.astype(F32)), c_blocks(-c_im.astype(F32)),
            lanes(lbr), lanes(lbi))


def _mix_out_kernel(with_router, h_ref, u_ref, yf_ref, yb_ref, a_ref, ap_ref, an_ref, mod_ref,
                    dsk_ref, wglu_ref, bglu_ref, cw_ref, cb_ref, lng_ref, lnb_ref, wout_ref,
                    gpost_ref, gffn_ref, *rest):
    if with_router:
        wrh_ref, wrl_ref, br_ref, hn_ref, f_ref, route_ref, pad, cvs = rest
    else:
        hn_ref, f_ref, pad, cvs = rest
    i = pl.program_id(0)
    nct = N_CTX // TM
    nt = N_ROWS // TM
    first = jnp.logical_or(i == 0, i == nct)
    last = jnp.logical_or(i == nct - 1, i == nt - 1)
    edge = CONV_PAD * BATCH

    y = yf_ref[...] + yb_ref[...] + dsk_ref[...] * u_ref[...]
    z = jax.nn.gelu(y)
    s = z * jax.nn.sigmoid(_dot(z.astype(BF16), wglu_ref[0]) + bglu_ref[...])

    pad[0:edge, :] = jnp.where(first, 0.0, ap_ref[HALO - edge:HALO, :])
    pad[edge:edge + TM, :] = a_ref[...]
    pad[edge + TM:2 * edge + TM, :] = jnp.where(last, 0.0, an_ref[0:edge, :])

    def conv_block(j, carry):
        r0 = pl.multiple_of(j * CONV_SUB, CONV_SUB)
        ng = CONV_SUB // BATCH
        step = lambda s: pad[pl.ds(r0 + s * BATCH, BATCH), :]
        win = [step(s) for s in range(ng)]
        acc = [jnp.zeros((BATCH, D_CONV), F32) for _ in range(ng)]
        for k in range(CONV_WIDTH):
            w = cw_ref[k]
            acc = [acc[q] + win[q] * w for q in range(ng)]
            if k + 1 < CONV_WIDTH:
                win = win[1:] + [step(k + ng)]
        for q in range(ng):
            cvs[pl.ds(r0 + q * BATCH, BATCH), :] = acc[q]
        return carry

    lax.fori_loop(0, TM // CONV_SUB, conv_block, 0)
    cv = cvs[...] + cb_ref[...]
    mu = jnp.mean(cv, axis=-1, keepdims=True)
    xc = cv - mu
    var = jnp.mean(xc * xc, axis=-1, keepdims=True)
    cv = xc * lax.rsqrt(var + EPS) * lng_ref[...] + lnb_ref[...]
    cv = cv * jax.nn.sigmoid(cv)

    out = _dot(s.astype(BF16), wout_ref[0, 0:D_SSM, :]) + _dot(cv.astype(BF16), wout_ref[0, D_SSM:, :])
    hn = _tile_gate_add(h_ref[...], mod_ref[0, 2], _rms(out, gpost_ref[...]))
    hn_ref[...] = hn
    f = _tile_mul_add(_rms(hn, gffn_ref[...]), mod_ref[0, 4], mod_ref[0, 3])
    if with_router:
        f_ref[...] = f.reshape(f_ref.shape)
    else:
        f_ref[...] = f.astype(BF16)

    if with_router:
        f_hi = f.astype(BF16)
        f_lo = (f - f_hi.astype(F32)).astype(BF16)
        logits = (_dot(f_hi, wrh_ref[...]) + _dot(f_lo, wrh_ref[...]) + _dot(f_hi, wrl_ref[...])
                  + br_ref[...])
        lane = lax.broadcasted_iota(jnp.int32, logits.shape, 1)
        neg = jnp.float32(-jnp.inf)
        logits = jnp.where(lane < N_EXPERTS, logits, neg)
        m1 = jnp.max(logits, axis=-1, keepdims=True)
        i1 = jnp.min(jnp.where(logits == m1, lane, LANES), axis=-1, keepdims=True)
        rest_l = jnp.where(lane == i1, neg, logits)
        m2 = jnp.max(rest_l, axis=-1, keepdims=True)
        i2 = jnp.min(jnp.where(rest_l == m2, lane, LANES), axis=-1, keepdims=True)
        e2 = jnp.exp(m2 - m1)
        den = 1.0 + e2
        route = jnp.where(lane == ROUTE_E1, i1.astype(F32), 0.0)
        route = jnp.where(lane == ROUTE_E2, i2.astype(F32), route)
        route = jnp.where(lane == ROUTE_W1, 1.0 / den, route)
        route_ref[...] = jnp.where(lane == ROUTE_W2, e2 / den, route)


def _mix_out_call(hs, u, yf, yb, agl, mod, p, layer, router=None):
    nt = N_ROWS // TM
    hb = TM // HALO
    nh = N_ROWS // HALO
    row = lambda i: (i, 0)
    in_specs = [
        pl.BlockSpec((TM, D_MODEL), row),
        pl.BlockSpec((TM, D_SSM), row),
        pl.BlockSpec((TM, D_SSM), row),
        pl.BlockSpec((TM, D_SSM), row),
        pl.BlockSpec((TM, D_CONV), row),
        pl.BlockSpec((HALO, D_CONV), lambda i: (jnp.maximum(i * hb - 1, 0), 0)),
        pl.BlockSpec((HALO, D_CONV), lambda i: (jnp.minimum((i + 1) * hb, nh - 1), 0)),
        pl.BlockSpec((1, N_MOD, SUBLANES, D_MODEL), lambda i: (_seg_index(i, 0), 0, 0, 0)),
        _const_spec((1, D_SSM)),
        _layer_spec((D_SSM, D_SSM), layer),
        _const_spec((1, D_SSM)),
        _const_spec((CONV_WIDTH, SUBLANES, D_CONV)),
        _const_spec((1, D_CONV)),
        _const_spec((1, D_CONV)),
        _const_spec((1, D_CONV)),
        _layer_spec((D_MODEL, D_MODEL), layer),
        _const_spec((1, D_MODEL)),
        _const_spec((1, D_MODEL)),
    ]
    args = [hs, u, yf, yb, agl, agl, agl, mod, p["ssm_d"], p["w_glu"], p["b_glu"], p["conv_w"],
            p["conv_b"], p["ln_g"], p["ln_b"], p["w_out"], p["g_post_mix"], p["g_pre_ffn"]]
    out_specs = [pl.BlockSpec((TM, D_MODEL), row)]
    out_shape = [jax.ShapeDtypeStruct((N_ROWS, D_MODEL), F32)]
    if router is None:
        out_specs.append(pl.BlockSpec((TM, D_MODEL), row))
        out_shape.append(jax.ShapeDtypeStruct((N_ROWS, D_MODEL), BF16))
    else:
        in_specs += [_const_spec((D_MODEL, LANES)), _const_spec((D_MODEL, LANES)), _const_spec((1, LANES))]
        args += list(router)
        out_specs += [pl.BlockSpec((TM,) + TOKEN_TILE, lambda i: (i, 0, 0)), pl.BlockSpec((TM, LANES), row)]
        out_shape += [jax.ShapeDtypeStruct((N_ROWS,) + TOKEN_TILE, F32),
                      jax.ShapeDtypeStruct((N_ROWS, LANES), F32)]
    return pl.pallas_call(
        functools.partial(_mix_out_kernel, router is not None),
        grid=(nt,),
        in_specs=in_specs,
        out_specs=out_specs,
        out_shape=out_shape,
        scratch_shapes=[
            pltpu.VMEM((TM + 2 * CONV_PAD * BATCH, D_CONV), F32),
            pltpu.VMEM((TM, D_CONV), F32),
        ],
        compiler_params=_params(("arbitrary",)),
        name="mix_out",
    )(*args)


def _ffn_kernel(f_ref, h_ref, mod_ref, wg_ref, wu_ref, wd_ref, gpost_ref, o_ref):
    f = f_ref[...]
    acc = None
    for j in range(D_FF // FF_CHUNK):
        cols = slice(j * FF_CHUNK, (j + 1) * FF_CHUNK)
        g = _dot(f, wg_ref[0, :, cols])
        a = (g * jax.nn.sigmoid(g)) * _dot(f, wu_ref[0, :, cols])
        part = _dot(a.astype(BF16), wd_ref[0, cols, :])
        acc = part if acc is None else acc + part
    o_ref[...] = _tile_gate_add(h_ref[...], mod_ref[0, 5], _rms(acc, gpost_ref[...]))


def _ffn_call(f, hs, mod, wg, wu, wd, g_post, layer):
    nt = N_ROWS // TM
    row = lambda i: (i, 0)
    once = dict(pipeline_mode=pl.Buffered(1))
    return pl.pallas_call(
        _ffn_kernel,
        grid=(nt,),
        in_specs=[
            pl.BlockSpec((TM, D_MODEL), row),
            pl.BlockSpec((TM, D_MODEL), row),
            pl.BlockSpec((1, N_MOD, SUBLANES, D_MODEL), lambda i: (_seg_index(i, 0), 0, 0, 0)),
            _layer_spec((D_MODEL, D_FF), layer, **once),
            _layer_spec((D_MODEL, D_FF), layer, **once),
            _layer_spec((D_FF, D_MODEL), layer, **once),
            _const_spec((1, D_MODEL)),
        ],
        out_specs=pl.BlockSpec((TM, D_MODEL), row),
        out_shape=jax.ShapeDtypeStruct((N_ROWS, D_MODEL), F32),
        compiler_params=_params(("arbitrary",)),
        name="ffn_dense",
    )(f, hs, mod, wg, wu, wd, g_post)


def _dispatch_plan(route, row0):
    r = route[row0:]
    e = jnp.concatenate([r[:, ROUTE_E1], r[:, ROUTE_E2]]).astype(jnp.int32)
    ids = jnp.arange(N_EXPERTS, dtype=jnp.int32)
    onehot = (e[:, None] == ids[None, :]).astype(jnp.int32)
    csum = jnp.cumsum(onehot, axis=0)
    counts = csum[-1]
    rank = jnp.sum((csum - onehot) * onehot, axis=1)
    padded = ((counts + MOE_TM - 1) // MOE_TM) * MOE_TM
    ends = jnp.cumsum(padded)
    offs = ends - padded
    pos = jnp.sum(onehot * offs[None, :], axis=1) + rank

    padlen = padded - counts
    cp_end = jnp.cumsum(padlen)
    cp = cp_end - padlen
    j = jnp.arange(MOE_PAD, dtype=jnp.int32)
    ej = jnp.minimum(jnp.sum((j[:, None] >= cp_end[None, :]).astype(jnp.int32), axis=1), N_EXPERTS - 1)
    pad_pos = jnp.where(j < cp_end[-1], offs[ej] + counts[ej] + (j - cp[ej]), ends[-1] + (j - cp_end[-1]))

    n_tiles = (2 * r.shape[0] + MOE_PAD) // MOE_TM
    starts = jnp.arange(n_tiles, dtype=jnp.int32) * MOE_TM
    tile_expert = jnp.minimum(jnp.sum((starts[:, None] >= ends[None, :]).astype(jnp.int32), axis=1),
                              N_EXPERTS - 1)
    n_used = (ends[-1] // MOE_TM).reshape(1)
    return pos, pad_pos.astype(jnp.int32), tile_expert, n_used


def _dispatch_kernel(pos_ref, pad_ref, f_ref, xs_ref, sem, *, n_tok):
    i = pl.program_id(0)

    def wait_tile():
        pltpu.make_async_copy(f_ref, xs_ref.at[pl.ds(0, TM)], sem).wait()

    def scatter(k8, carry):
        for j in range(DMA_UNROLL):
            k = k8 * DMA_UNROLL + j
            tok = i * TM + k
            pltpu.make_async_copy(f_ref.at[k], xs_ref.at[pos_ref[tok]], sem).start(priority=0)
            pltpu.make_async_copy(f_ref.at[k], xs_ref.at[pos_ref[n_tok + tok]], sem).start(priority=1)
        return carry

    lax.fori_loop(0, TM // DMA_UNROLL, scatter, 0)

    @pl.when(i == 0)
    def _():
        def fill(k8, carry):
            for j in range(DMA_UNROLL):
                pltpu.make_async_copy(f_ref.at[0], xs_ref.at[pad_ref[k8 * DMA_UNROLL + j]],
                                      sem).start(priority=j % 2)
            return carry
        lax.fori_loop(0, MOE_PAD // DMA_UNROLL, fill, 0)
        for _ in range(MOE_PAD // TM):
            wait_tile()

    wait_tile()
    wait_tile()


def _dispatch_call(pos, pad_pos, f, row0):
    n_tok = N_ROWS - row0
    off = row0 // TM
    return pl.pallas_call(
        functools.partial(_dispatch_kernel, n_tok=n_tok),
        grid_spec=pltpu.PrefetchScalarGridSpec(
            num_scalar_prefetch=2,
            grid=(n_tok // TM,),
            in_specs=[pl.BlockSpec((TM,) + TOKEN_TILE, lambda i, pos, pad: (i + off, 0, 0))],
            out_specs=pl.BlockSpec(memory_space=pl.ANY),
            scratch_shapes=[pltpu.SemaphoreType.DMA(())],
        ),
        out_shape=jax.ShapeDtypeStruct((2 * n_tok + MOE_PAD,) + TOKEN_TILE, F32),
        compiler_params=_params(("arbitrary",)),
        name="moe_dispatch",
    )(pos, pad_pos, f)


def _moe_group_kernel(te_ref, nu_ref, x_ref, wg_ref, wu_ref, wd_ref, o_ref):
    p = pl.program_id(0)
    j = pl.program_id(1)

    @pl.when(p < nu_ref[0])
    def _():
        x = x_ref[...].reshape(MOE_TM, D_MODEL).astype(BF16)
        g = _dot(x, wg_ref[0, 0])
        a = (g * jax.nn.sigmoid(g)) * _dot(x, wu_ref[0, 0])
        part = _dot(a.astype(BF16), wd_ref[0, 0]).reshape(o_ref.shape)

        @pl.when(j == 0)
        def _():
            o_ref[...] = part

        @pl.when(j > 0)
        def _():
            o_ref[...] += part

    @pl.when(p >= nu_ref[0])
    def _():
        o_ref[...] = jnp.zeros_like(o_ref)


def _moe_group_call(tile_expert, n_used, xs, wg, wu, wd, layer):
    n_tiles = xs.shape[0] // MOE_TM
    nf = D_FF // FF_CHUNK
    chunk = lambda p, j, nu: jnp.where(p < nu[0], j, nf - 1)
    return pl.pallas_call(
        _moe_group_kernel,
        grid_spec=pltpu.PrefetchScalarGridSpec(
            num_scalar_prefetch=2,
            grid=(n_tiles, nf),
            in_specs=[
                pl.BlockSpec((MOE_TM,) + TOKEN_TILE, lambda p, j, te, nu: (p, 0, 0)),
                pl.BlockSpec((1, 1, D_MODEL, FF_CHUNK),
                             lambda p, j, te, nu: (layer, te[p], 0, chunk(p, j, nu))),
                pl.BlockSpec((1, 1, D_MODEL, FF_CHUNK),
                             lambda p, j, te, nu: (layer, te[p], 0, chunk(p, j, nu))),
                pl.BlockSpec((1, 1, FF_CHUNK, D_MODEL),
                             lambda p, j, te, nu: (layer, te[p], chunk(p, j, nu), 0)),
            ],
            out_specs=pl.BlockSpec((MOE_TM,) + TOKEN_TILE, lambda p, j, te, nu: (p, 0, 0)),
        ),
        out_shape=jax.ShapeDtypeStruct(xs.shape, F32),
        compiler_params=_params(("arbitrary", "arbitrary")),
        name="moe_group",
    )(tile_expert, n_used, xs, wg, wu, wd)


def _moe_combine_kernel(to_output, n_tok, pos_ref, ys_ref, route_ref, h_ref, mod_ref, gpost_ref, o_ref,
                        ybuf, sem):
    i = pl.program_id(0)

    def request(tile):
        b = tile % 2

        def gather(k8, carry):
            for j in range(DMA_UNROLL):
                k = k8 * DMA_UNROLL + j
                tok = tile * TM + k
                pltpu.make_async_copy(ys_ref.at[pos_ref[tok]], ybuf.at[b, 0, k], sem.at[b]).start(priority=0)
                pltpu.make_async_copy(ys_ref.at[pos_ref[n_tok + tok]], ybuf.at[b, 1, k],
                                      sem.at[b]).start(priority=1)
            return carry

        lax.fori_loop(0, TM // DMA_UNROLL, gather, 0)

    @pl.when(i == 0)
    def _():
        request(i)

    @pl.when(i + 1 < pl.num_programs(0))
    def _():
        request(i + 1)

    cur = i % 2
    for slot in range(2):
        pltpu.make_async_copy(ys_ref.at[pl.ds(0, TM)], ybuf.at[cur, slot], sem.at[cur]).wait()

    route = route_ref[...]
    lane = lax.broadcasted_iota(jnp.int32, route.shape, 1)
    w1 = jnp.sum(jnp.where(lane == ROUTE_W1, route, 0.0), axis=-1, keepdims=True)
    w2 = jnp.sum(jnp.where(lane == ROUTE_W2, route, 0.0), axis=-1, keepdims=True)
    y = w1 * ybuf[cur, 0].reshape(TM, D_MODEL) + w2 * ybuf[cur, 1].reshape(TM, D_MODEL)
    hn = _tile_gate_add(h_ref[...], mod_ref[0, 5], _rms(y, gpost_ref[...]))
    o_ref[...] = _from_stream(hn) if to_output else hn


def _moe_combine_call(pos, ys, route, hs, mod, g_post, row0, to_output):
    t = N_ROWS - row0
    nt = t // TM
    off = row0 // TM
    if to_output:
        out_spec = pl.BlockSpec((BATCH, TM // BATCH, D_MODEL), lambda i, pos: (0, i, 0))
        out_shape = jax.ShapeDtypeStruct((BATCH, t // BATCH, D_MODEL), F32)
    else:
        out_spec = pl.BlockSpec((TM, D_MODEL), lambda i, pos: (i, 0))
        out_shape = jax.ShapeDtypeStruct((t, D_MODEL), F32)
    return pl.pallas_call(
        functools.partial(_moe_combine_kernel, to_output, t),
        grid_spec=pltpu.PrefetchScalarGridSpec(
            num_scalar_prefetch=1,
            grid=(nt,),
            in_specs=[
                pl.BlockSpec(memory_space=pl.ANY),
                pl.BlockSpec((TM, LANES), lambda i, pos: (i + off, 0)),
                pl.BlockSpec((TM, D_MODEL), lambda i, pos: (i + off, 0)),
                pl.BlockSpec((1, N_MOD, SUBLANES, D_MODEL), lambda i, pos: (_seg_index(i, off), 0, 0, 0)),
                pl.BlockSpec((1, D_MODEL), lambda i, pos: (0, 0)),
            ],
            out_specs=out_spec,
            scratch_shapes=[pltpu.VMEM((2, 2, TM) + TOKEN_TILE, F32), pltpu.SemaphoreType.DMA((2,))],
        ),
        out_shape=out_shape,
        compiler_params=_params(("arbitrary",)),
        name="moe_combine",
    )(pos, ys, route, hs, mod, g_post)


def _moe_call(f, hs, route, mod, wg, wu, wd, g_post, row0, layer):
    pos, pad_pos, tile_expert, n_used = _dispatch_plan(route, row0)
    xs = _dispatch_call(pos, pad_pos, f, row0)
    ys = _moe_group_call(tile_expert, n_used, xs, wg, wu, wd, layer)
    return _moe_combine_call(pos, ys, route, hs, mod, g_post, row0, to_output=row0 > 0)


def kernel(x, c, ctx, c_ctx, w_ada, b_ada, g_pre_mix, g_post_mix, g_pre_ffn, g_post_ffn, w_in,
           ssm_lam_re, ssm_lam_im, ssm_log_dt, ssm_b_re, ssm_b_im, ssm_c_re, ssm_c_im, ssm_d,
           ssm_w_glu, ssm_b_glu, conv_w, conv_b, conv_ln_g, conv_ln_b, w_out,
           ffn_w_gate, ffn_w_up, ffn_w_down, moe_w_router, moe_b_router, moe_w_gate, moe_w_up, moe_w_down):
    cin = jnp.concatenate([jnp.broadcast_to(c_ctx[None, :], (BATCH, D_MODEL)), c], axis=0)
    mod_all = _ada_call(cin, w_ada, b_ada)
    mod_all = mod_all.reshape(DEPTH, 2, BATCH, N_MOD, D_MODEL).transpose(0, 1, 3, 2, 4)

    w_in_b, w_glu_b, w_out_b = w_in.astype(BF16), ssm_w_glu.astype(BF16), w_out.astype(BF16)
    ffn_b = (ffn_w_gate.astype(BF16), ffn_w_up.astype(BF16), ffn_w_down.astype(BF16))
    moe_b = (moe_w_gate.astype(BF16), moe_w_up.astype(BF16), moe_w_down.astype(BF16))
    s5 = jax.vmap(_s5_params)(ssm_lam_re, ssm_lam_im, ssm_log_dt, ssm_b_re, ssm_b_im, ssm_c_re, ssm_c_im)

    row_vec = lambda v: v.reshape(1, -1).astype(F32)
    for l in range(DEPTH):
        mod = mod_all[l]
        if l == 0:
            u, agl, hs = _mix_in_call((ctx, x), mod, row_vec(g_pre_mix[l]), w_in_b, l)
        else:
            u, agl = _mix_in_call(hs, mod, row_vec(g_pre_mix[l]), w_in_b, l)
        yf, yb = _scan_call(u, l, *s5)
        p = dict(
            ssm_d=row_vec(ssm_d[l]), w_glu=w_glu_b, b_glu=row_vec(ssm_b_glu[l]),
            conv_w=jnp.broadcast_to(conv_w[l][:, None, :], (CONV_WIDTH, SUBLANES, D_CONV)).astype(F32),
            conv_b=row_vec(conv_b[l]), ln_g=row_vec(conv_ln_g[l]), ln_b=row_vec(conv_ln_b[l]),
            w_out=w_out_b, g_post_mix=row_vec(g_post_mix[l]), g_pre_ffn=row_vec(g_pre_ffn[l]))
        i = l // 2
        if l % 2 == 0:
            hs, f = _mix_out_call(hs, u, yf, yb, agl, mod, p, l)
            hs = _ffn_call(f, hs, mod, *ffn_b, row_vec(g_post_ffn[l]), i)
        else:
            wr = jnp.zeros((D_MODEL, LANES), F32).at[:, :N_EXPERTS].set(moe_w_router[i].astype(F32))
            wr_hi = wr.astype(BF16)
            wr_lo = (wr - wr_hi.astype(F32)).astype(BF16)
            br = jnp.zeros((1, LANES), F32).at[0, :N_EXPERTS].set(moe_b_router[i].astype(F32))
            hs, f, route = _mix_out_call(hs, u, yf, yb, agl, mod, p, l, router=(wr_hi, wr_lo, br))
            row0 = 0 if l < DEPTH - 1 else N_CTX
            hs = _moe_call(f, hs, route, mod, *moe_b, row_vec(g_post_ffn[l]), row0, i)
    return hs
```

```python
import functools
import math

import jax
import jax.numpy as jnp
from jax import lax
from jax.experimental import pallas as pl
from jax.experimental.pallas import tpu as pltpu

F32 = jnp.float32
BF16 = jnp.bfloat16

D_MODEL = 1024
BATCH = 8
SEQ = 2048
CTX_LEN = 256
DEPTH = 4
D_SSM = 512
SSM_GROUP = 16
N_SSM_GROUPS = 32
SSM_STATE = 64
D_CONV = 512
CONV_WIDTH = 31
CONV_PAD = CONV_WIDTH // 2
D_IN = D_SSM + 2 * D_CONV
D_FF = 2816
N_EXPERTS = 8
N_MOD = 6
EPS = 1e-6

N_CTX = CTX_LEN * BATCH
N_LAT = SEQ * BATCH
N_ROWS = N_CTX + N_LAT
N_STATE = N_SSM_GROUPS * SSM_STATE

SUBLANES = 8
LANES = 128
VMEM_LIMIT = 56 * 1024 * 1024

TM = 512
SCAN_T = 64
SCAN_R = SCAN_T * BATCH
SCAN_SLABS = 2
SLAB_CH = D_SSM // SCAN_SLABS
SLAB_ST = N_STATE // SCAN_SLABS
B_SLABS = 2
B_SLAB_CH = D_SSM // B_SLABS
B_SLAB_ST = N_STATE // B_SLABS
SCAN_W = 512
HALO = 128
CONV_SUB = 32
FF_CHUNK = 1408
MOE_TM = 512
MOE_PAD = N_EXPERTS * MOE_TM
DMA_UNROLL = 8
ROUTE_E1, ROUTE_E2, ROUTE_W1, ROUTE_W2 = 0, 1, 2, 3
TOKEN_TILE = (SUBLANES, D_MODEL // SUBLANES)


def _dot(a, b):
    return jnp.dot(a, b, preferred_element_type=F32)


def _rms(x, g):
    ms = jnp.mean(x * x, axis=-1, keepdims=True)
    return x * lax.rsqrt(ms + EPS) * g


def _tile_mul_add(y, scale, shift):
    rows, d = y.shape
    y3 = y.reshape(rows // SUBLANES, SUBLANES, d)
    return (y3 * (1.0 + scale)[None] + shift[None]).reshape(rows, d)


def _tile_gate_add(h, gate, r):
    rows, d = h.shape
    r3 = r.reshape(rows // SUBLANES, SUBLANES, d)
    return h + (gate[None] * r3).reshape(rows, d)


def _params(sem):
    return pltpu.CompilerParams(dimension_semantics=sem, vmem_limit_bytes=VMEM_LIMIT)


def _const_spec(shape):
    nd = len(shape)
    return pl.BlockSpec(shape, lambda *_: (0,) * nd)


def _ada_kernel(c_ref, w_ref, b_ref, o_ref):
    c = c_ref[...]
    s = c * jax.nn.sigmoid(c)
    o_ref[0] = _dot(s, w_ref[0]) + b_ref[0]


def _ada_call(cin, w_ada, b_ada):
    tn = 1536
    return pl.pallas_call(
        _ada_kernel,
        grid=(DEPTH, N_MOD * D_MODEL // tn),
        in_specs=[
            pl.BlockSpec((2 * BATCH, D_MODEL), lambda l, j: (0, 0)),
            pl.BlockSpec((1, D_MODEL, tn), lambda l, j: (l, 0, j)),
            pl.BlockSpec((1, 1, tn), lambda l, j: (l, 0, j)),
        ],
        out_specs=pl.BlockSpec((1, 2 * BATCH, tn), lambda l, j: (l, 0, j)),
        out_shape=jax.ShapeDtypeStruct((DEPTH, 2 * BATCH, N_MOD * D_MODEL), F32),
        compiler_params=_params(("arbitrary", "arbitrary")),
        name="ada_mod",
    )(cin, w_ada, b_ada.reshape(DEPTH, 1, N_MOD * D_MODEL))


def _to_stream(x):
    return jnp.swapaxes(x, 0, 1).reshape(x.shape[1] * BATCH, x.shape[2])


def _from_stream(h):
    return jnp.swapaxes(h.reshape(h.shape[0] // BATCH, BATCH, h.shape[1]), 0, 1)


def _mix_in_kernel(from_inputs, *refs):
    if from_inputs:
        ctx_ref, x_ref, mod_ref, g_ref, w_ref, u_ref, a_ref, hs_ref = refs
        is_ctx = pl.program_id(0) < N_CTX // TM

        @pl.when(is_ctx)
        def _():
            hs_ref[...] = _to_stream(ctx_ref[...])

        @pl.when(jnp.logical_not(is_ctx))
        def _():
            hs_ref[...] = _to_stream(x_ref[...])

        h = hs_ref[...]
    else:
        h_ref, mod_ref, g_ref, w_ref, u_ref, a_ref = refs
        h = h_ref[...]
    y = _rms(h, g_ref[...])
    a = _tile_mul_add(y, mod_ref[0, 1], mod_ref[0, 0]).astype(BF16)
    p = _dot(a, w_ref[0])
    u_ref[...] = p[:, :D_SSM]
    v = p[:, D_SSM:D_SSM + D_CONV]
    g = p[:, D_SSM + D_CONV:]
    a_ref[...] = v * jax.nn.sigmoid(g)


def _seg_index(i, off):
    return ((i + off) >= (N_CTX // TM)).astype(jnp.int32)


def _layer_spec(shape, layer, **kwargs):
    return pl.BlockSpec((1,) + shape, lambda *_: (layer,) + (0,) * len(shape), **kwargs)


def _mix_in_call(hs, mod, g_pre, w_in, layer):
    nt = N_ROWS // TM
    from_inputs = isinstance(hs, tuple)
    row = lambda i: (i, 0)
    if from_inputs:
        steps = TM // BATCH
        nct = N_CTX // TM
        h_specs = [pl.BlockSpec((BATCH, steps, D_MODEL), lambda i: (0, jnp.minimum(i, nct - 1), 0)),
                   pl.BlockSpec((BATCH, steps, D_MODEL), lambda i: (0, jnp.maximum(i - nct, 0), 0))]
        h_args = list(hs)
    else:
        h_specs = [pl.BlockSpec((TM, D_MODEL), row)]
        h_args = [hs]
    out_specs = [pl.BlockSpec((TM, D_SSM), row), pl.BlockSpec((TM, D_CONV), row)]
    out_shape = [jax.ShapeDtypeStruct((N_ROWS, D_SSM), F32), jax.ShapeDtypeStruct((N_ROWS, D_CONV), F32)]
    if from_inputs:
        out_specs.append(pl.BlockSpec((TM, D_MODEL), row))
        out_shape.append(jax.ShapeDtypeStruct((N_ROWS, D_MODEL), F32))
    return pl.pallas_call(
        functools.partial(_mix_in_kernel, from_inputs),
        grid=(nt,),
        in_specs=h_specs + [
            pl.BlockSpec((1, N_MOD, SUBLANES, D_MODEL), lambda i: (_seg_index(i, 0), 0, 0, 0)),
            _const_spec((1, D_MODEL)),
            _layer_spec((D_MODEL, D_IN), layer),
        ],
        out_specs=out_specs,
        out_shape=out_shape,
        compiler_params=_params(("arbitrary",)),
        name="mix_in",
    )(*h_args, mod, g_pre, w_in)


def _scan_phases(refs, cur, prv, st_re, st_im):
    uf_ref, ub_ref, bre_ref, bim_ref, cre_ref, cim_ref, lr_ref, li_ref, yf_ref, yb_ref = refs
    cur_re, cur_im = cur
    prv_re, prv_im = prv

    for d, y_ref in enumerate((yf_ref, yb_ref)):
        for s in range(SCAN_SLABS):
            st = slice(s * SLAB_ST, (s + 1) * SLAB_ST)
            y_ref[:, s * SLAB_CH:(s + 1) * SLAB_CH] = (
                _dot(cur_re[d, :, st].astype(BF16), cre_ref[0, d, s])
                + _dot(cur_im[d, :, st].astype(BF16), cim_ref[0, d, s]))

    for d, u_ref in enumerate((uf_ref, ub_ref)):
        u = u_ref[...].astype(BF16)
        for s in range(B_SLABS):
            us = u[:, s * B_SLAB_CH:(s + 1) * B_SLAB_CH]
            cur_re[d, :, s * B_SLAB_ST:(s + 1) * B_SLAB_ST] = _dot(us, bre_ref[0, d, s])
            cur_im[d, :, s * B_SLAB_ST:(s + 1) * B_SLAB_ST] = _dot(us, bim_ref[0, d, s])

    for d in range(2):
        for c in range(N_STATE // SCAN_W):
            cols = slice(c * SCAN_W, (c + 1) * SCAN_W)
            lam_r = lr_ref[0, d, :, cols]
            lam_i = li_ref[0, d, :, cols]
            sr = st_re[d, :, cols]
            si = st_im[d, :, cols]
            for t in range(SCAN_T):
                tt = t if d == 0 else SCAN_T - 1 - t
                rows = slice(tt * SUBLANES, (tt + 1) * SUBLANES)
                nr = lam_r * sr - lam_i * si + prv_re[d, rows, cols]
                ni = lam_r * si + lam_i * sr + prv_im[d, rows, cols]
                prv_re[d, rows, cols] = nr
                prv_im[d, rows, cols] = ni
                sr, si = nr, ni
            st_re[d, :, cols] = sr
            st_im[d, :, cols] = si


def _scan_kernel(*refs):
    io, (re0, im0, re1, im1, st_re, st_im) = refs[:10], refs[10:]
    k = pl.program_id(0)

    @pl.when(k == 0)
    def _():
        for buf in (re0, im0, re1, im1, st_re, st_im):
            buf[...] = jnp.zeros_like(buf)

    @pl.when(k % 2 == 0)
    def _():
        _scan_phases(io, (re0, im0), (re1, im1), st_re, st_im)

    @pl.when(k % 2 == 1)
    def _():
        _scan_phases(io, (re1, im1), (re0, im0), st_re, st_im)


def _bwd_chunk(k):
    nc = N_CTX // SCAN_R
    n = N_ROWS // SCAN_R
    return jnp.where(k < nc, nc - 1 - k, n - 1 + nc - k)


def _scan_call(u, layer, bre, bim, cre, cim, lam_r, lam_i):
    n = N_ROWS // SCAN_R
    bu_chunk = lambda k: jnp.minimum(k, n - 1)
    ch_chunk = lambda k: jnp.maximum(k - 2, 0)
    once = dict(pipeline_mode=pl.Buffered(1))
    return pl.pallas_call(
        _scan_kernel,
        grid=(n + 2,),
        in_specs=[
            pl.BlockSpec((SCAN_R, D_SSM), lambda k: (bu_chunk(k), 0)),
            pl.BlockSpec((SCAN_R, D_SSM), lambda k: (_bwd_chunk(bu_chunk(k)), 0)),
            _layer_spec((2, B_SLABS, B_SLAB_CH, B_SLAB_ST), layer, **once),
            _layer_spec((2, B_SLABS, B_SLAB_CH, B_SLAB_ST), layer, **once),
            _layer_spec((2, SCAN_SLABS, SLAB_ST, SLAB_CH), layer, **once),
            _layer_spec((2, SCAN_SLABS, SLAB_ST, SLAB_CH), layer, **once),
            _layer_spec((2, SUBLANES, N_STATE), layer, **once),
            _layer_spec((2, SUBLANES, N_STATE), layer, **once),
        ],
        out_specs=[
            pl.BlockSpec((SCAN_R, D_SSM), lambda k: (ch_chunk(k), 0)),
            pl.BlockSpec((SCAN_R, D_SSM), lambda k: (_bwd_chunk(ch_chunk(k)), 0)),
        ],
        out_shape=[
            jax.ShapeDtypeStruct((N_ROWS, D_SSM), F32),
            jax.ShapeDtypeStruct((N_ROWS, D_SSM), F32),
        ],
        scratch_shapes=[pltpu.VMEM((2, SCAN_R, N_STATE), F32)] * 4
        + [pltpu.VMEM((2, SUBLANES, N_STATE), F32)] * 2,
        compiler_params=_params(("arbitrary",)),
        name="s5_scan",
    )(u, u, bre, bim, cre, cim, lam_r, lam_i)


def _s5_params(lam_re, lam_im, log_dt, b_re, b_im, c_re, c_im):
    lr, li = lam_re.astype(F32), lam_im.astype(F32)
    dt = jnp.exp(log_dt.astype(F32))[..., None]
    mag = jnp.exp(lr * dt)
    lbr, lbi = mag * jnp.cos(li * dt), mag * jnp.sin(li * dt)
    den = lr * lr + li * li
    qr = ((lbr - 1.0) * lr + lbi * li) / den
    qi = (lbi * lr - (lbr - 1.0) * li) / den
    br, bi = b_re.astype(F32), b_im.astype(F32)
    bbr = qr[..., None] * br - qi[..., None] * bi
    bbi = qr[..., None] * bi + qi[..., None] * br
    def block_diag(m, slabs):
        gs = N_SSM_GROUPS // slabs
        a, b = m.shape[2:]
        rows = jnp.swapaxes(m.reshape(2, slabs, gs, a, b), 3, 4).reshape(2, slabs, gs * b, a)
        col = jnp.arange(gs * a)
        spread = (jnp.arange(a)[:, None] == col[None, :] % a).astype(F32)
        own = (jnp.arange(gs * b)[:, None] // b) == (col[None, :] // a)
        tiled = jnp.einsum('dsra,ac->dsrc', rows, spread, precision=lax.Precision.HIGHEST)
        return jnp.where(own, tiled, 0.0).astype(BF16)

    def b_blocks(m):
        return block_diag(m, B_SLABS)

    def c_blocks(m):
        return block_diag(m, SCAN_SLABS)

    def lanes(v):
        return jnp.broadcast_to(v.reshape(2, 1, N_STATE), (2, SUBLANES, N_STATE))

    return (b_blocks(bbr), b_blocks(bbi), c_blocks(c_re.astype(F32)), c_blocks(-c_im.astype(F32)),
            lanes(lbr), lanes(lbi))


def _mix_out_kernel(with_router, h_ref, u_ref, yf_ref, yb_ref, a_ref, ap_ref, an_ref, mod_ref,
                    dsk_ref, wglu_ref, bglu_ref, cw_ref, cb_ref, lng_ref, lnb_ref, wout_ref,
                    gpost_ref, gffn_ref, *rest):
    if with_router:
        wrh_ref, wrl_ref, br_ref, hn_ref, f_ref, route_ref, pad, cvs = rest
    else:
        hn_ref, f_ref, pad, cvs = rest
    i = pl.program_id(0)
    nct = N_CTX // TM
    nt = N_ROWS // TM
    first = jnp.logical_or(i == 0, i == nct)
    last = jnp.logical_or(i == nct - 1, i == nt - 1)
    edge = CONV_PAD * BATCH

    y = yf_ref[...] + yb_ref[...] + dsk_ref[...] * u_ref[...]
    z = jax.nn.gelu(y)
    s = z * jax.nn.sigmoid(_dot(z.astype(BF16), wglu_ref[0]) + bglu_ref[...])

    pad[0:edge, :] = jnp.where(first, 0.0, ap_ref[HALO - edge:HALO, :])
    pad[edge:edge + TM, :] = a_ref[...]
    pad[edge + TM:2 * edge + TM, :] = jnp.where(last, 0.0, an_ref[0:edge, :])

    def conv_block(j, carry):
        r0 = pl.multiple_of(j * CONV_SUB, CONV_SUB)
        ng = CONV_SUB // BATCH
        step = lambda s: pad[pl.ds(r0 + s * BATCH, BATCH), :]
        win = [step(s) for s in range(ng)]
        acc = [jnp.zeros((BATCH, D_CONV), F32) for _ in range(ng)]
        for k in range(CONV_WIDTH):
            w = cw_ref[k]
            acc = [acc[q] + win[q] * w for q in range(ng)]
            if k + 1 < CONV_WIDTH:
                win = win[1:] + [step(k + ng)]
        for q in range(ng):
            cvs[pl.ds(r0 + q * BATCH, BATCH), :] = acc[q]
        return carry

    lax.fori_loop(0, TM // CONV_SUB, conv_block, 0)
    cv = cvs[...] + cb_ref[...]
    mu = jnp.mean(cv, axis=-1, keepdims=True)
    xc = cv - mu
    var = jnp.mean(xc * xc, axis=-1, keepdims=True)
    cv = xc * lax.rsqrt(var + EPS) * lng_ref[...] + lnb_ref[...]
    cv = cv * jax.nn.sigmoid(cv)

    out = _dot(s.astype(BF16), wout_ref[0, 0:D_SSM, :]) + _dot(cv.astype(BF16), wout_ref[0, D_SSM:, :])
    hn = _tile_gate_add(h_ref[...], mod_ref[0, 2], _rms(out, gpost_ref[...]))
    hn_ref[...] = hn
    f = _tile_mul_add(_rms(hn, gffn_ref[...]), mod_ref[0, 4], mod_ref[0, 3])
    if with_router:
        f_ref[...] = f.reshape(f_ref.shape)
    else:
        f_ref[...] = f.astype(BF16)

    if with_router:
        f_hi = f.astype(BF16)
        f_lo = (f - f_hi.astype(F32)).astype(BF16)
        logits = (_dot(f_hi, wrh_ref[...]) + _dot(f_lo, wrh_ref[...]) + _dot(f_hi, wrl_ref[...])
                  + br_ref[...])
        lane = lax.broadcasted_iota(jnp.int32, logits.shape, 1)
        neg = jnp.float32(-jnp.inf)
        logits = jnp.where(lane < N_EXPERTS, logits, neg)
        m1 = jnp.max(logits, axis=-1, keepdims=True)
        i1 = jnp.min(jnp.where(logits == m1, lane, LANES), axis=-1, keepdims=True)
        rest_l = jnp.where(lane == i1, neg, logits)
        m2 = jnp.max(rest_l, axis=-1, keepdims=True)
        i2 = jnp.min(jnp.where(rest_l == m2, lane, LANES), axis=-1, keepdims=True)
        e2 = jnp.exp(m2 - m1)
        den = 1.0 + e2
        route = jnp.where(lane == ROUTE_E1, i1.astype(F32), 0.0)
        route = jnp.where(lane == ROUTE_E2, i2.astype(F32), route)
        route = jnp.where(lane == ROUTE_W1, 1.0 / den, route)
        route_ref[...] = jnp.where(lane == ROUTE_W2, e2 / den, route)


def _mix_out_call(hs, u, yf, yb, agl, mod, p, layer, router=None):
    nt = N_ROWS // TM
    hb = TM // HALO
    nh = N_ROWS // HALO
    row = lambda i: (i, 0)
    in_specs = [
        pl.BlockSpec((TM, D_MODEL), row),
        pl.BlockSpec((TM, D_SSM), row),
        pl.BlockSpec((TM, D_SSM), row),
        pl.BlockSpec((TM, D_SSM), row),
        pl.BlockSpec((TM, D_CONV), row),
        pl.BlockSpec((HALO, D_CONV), lambda i: (jnp.maximum(i * hb - 1, 0), 0)),
        pl.BlockSpec((HALO, D_CONV), lambda i: (jnp.minimum((i + 1) * hb, nh - 1), 0)),
        pl.BlockSpec((1, N_MOD, SUBLANES, D_MODEL), lambda i: (_seg_index(i, 0), 0, 0, 0)),
        _const_spec((1, D_SSM)),
        _layer_spec((D_SSM, D_SSM), layer),
        _const_spec((1, D_SSM)),
        _const_spec((CONV_WIDTH, SUBLANES, D_CONV)),
        _const_spec((1, D_CONV)),
        _const_spec((1, D_CONV)),
        _const_spec((1, D_CONV)),
        _layer_spec((D_MODEL, D_MODEL), layer),
        _const_spec((1, D_MODEL)),
        _const_spec((1, D_MODEL)),
    ]
    args = [hs, u, yf, yb, agl, agl, agl, mod, p["ssm_d"], p["w_glu"], p["b_glu"], p["conv_w"],
            p["conv_b"], p["ln_g"], p["ln_b"], p["w_out"], p["g_post_mix"], p["g_pre_ffn"]]
    out_specs = [pl.BlockSpec((TM, D_MODEL), row)]
    out_shape = [jax.ShapeDtypeStruct((N_ROWS, D_MODEL), F32)]
    if router is None:
        out_specs.append(pl.BlockSpec((TM, D_MODEL), row))
        out_shape.append(jax.ShapeDtypeStruct((N_ROWS, D_MODEL), BF16))
    else:
        in_specs += [_const_spec((D_MODEL, LANES)), _const_spec((D_MODEL, LANES)), _const_spec((1, LANES))]
        args += list(router)
        out_specs += [pl.BlockSpec((TM,) + TOKEN_TILE, lambda i: (i, 0, 0)), pl.BlockSpec((TM, LANES), row)]
        out_shape += [jax.ShapeDtypeStruct((N_ROWS,) + TOKEN_TILE, F32),
                      jax.ShapeDtypeStruct((N_ROWS, LANES), F32)]
    return pl.pallas_call(
        functools.partial(_mix_out_kernel, router is not None),
        grid=(nt,),
        in_specs=in_specs,
        out_specs=out_specs,
        out_shape=out_shape,
        scratch_shapes=[
            pltpu.VMEM((TM + 2 * CONV_PAD * BATCH, D_CONV), F32),
            pltpu.VMEM((TM, D_CONV), F32),
        ],
        compiler_params=_params(("arbitrary",)),
        name="mix_out",
    )(*args)


def _ffn_kernel(f_ref, h_ref, mod_ref, wg_ref, wu_ref, wd_ref, gpost_ref, o_ref):
    f = f_ref[...]
    acc = None
    for j in range(D_FF // FF_CHUNK):
        cols = slice(j * FF_CHUNK, (j + 1) * FF_CHUNK)
        g = _dot(f, wg_ref[0, :, cols])
        a = (g * jax.nn.sigmoid(g)) * _dot(f, wu_ref[0, :, cols])
        part = _dot(a.astype(BF16), wd_ref[0, cols, :])
        acc = part if acc is None else acc + part
    o_ref[...] = _tile_gate_add(h_ref[...], mod_ref[0, 5], _rms(acc, gpost_ref[...]))


def _ffn_call(f, hs, mod, wg, wu, wd, g_post, layer):
    nt = N_ROWS // TM
    row = lambda i: (i, 0)
    once = dict(pipeline_mode=pl.Buffered(1))
    return pl.pallas_call(
        _ffn_kernel,
        grid=(nt,),
        in_specs=[
            pl.BlockSpec((TM, D_MODEL), row),
            pl.BlockSpec((TM, D_MODEL), row),
            pl.BlockSpec((1, N_MOD, SUBLANES, D_MODEL), lambda i: (_seg_index(i, 0), 0, 0, 0)),
            _layer_spec((D_MODEL, D_FF), layer, **once),
            _layer_spec((D_MODEL, D_FF), layer, **once),
            _layer_spec((D_FF, D_MODEL), layer, **once),
            _const_spec((1, D_MODEL)),
        ],
        out_specs=pl.BlockSpec((TM, D_MODEL), row),
        out_shape=jax.ShapeDtypeStruct((N_ROWS, D_MODEL), F32),
        compiler_params=_params(("arbitrary",)),
        name="ffn_dense",
    )(f, hs, mod, wg, wu, wd, g_post)


def _dispatch_plan(route, row0):
    r = route[row0:]
    e = jnp.concatenate([r[:, ROUTE_E1], r[:, ROUTE_E2]]).astype(jnp.int32)
    ids = jnp.arange(N_EXPERTS, dtype=jnp.int32)
    onehot = (e[:, None] == ids[None, :]).astype(jnp.int32)
    csum = jnp.cumsum(onehot, axis=0)
    counts = csum[-1]
    rank = jnp.sum((csum - onehot) * onehot, axis=1)
    padded = ((counts + MOE_TM - 1) // MOE_TM) * MOE_TM
    ends = jnp.cumsum(padded)
    offs = ends - padded
    pos = jnp.sum(onehot * offs[None, :], axis=1) + rank

    padlen = padded - counts
    cp_end = jnp.cumsum(padlen)
    cp = cp_end - padlen
    j = jnp.arange(MOE_PAD, dtype=jnp.int32)
    ej = jnp.minimum(jnp.sum((j[:, None] >= cp_end[None, :]).astype(jnp.int32), axis=1), N_EXPERTS - 1)
    pad_pos = jnp.where(j < cp_end[-1], offs[ej] + counts[ej] + (j - cp[ej]), ends[-1] + (j - cp_end[-1]))

    n_tiles = (2 * r.shape[0] + MOE_PAD) // MOE_TM
    starts = jnp.arange(n_tiles, dtype=jnp.int32) * MOE_TM
    tile_expert = jnp.minimum(jnp.sum((starts[:, None] >= ends[None, :]).astype(jnp.int32), axis=1),
                              N_EXPERTS - 1)
    n_used = (ends[-1] // MOE_TM).reshape(1)
    return pos, pad_pos.astype(jnp.int32), tile_expert, n_used


def _dispatch_kernel(pos_ref, pad_ref, f_ref, xs_ref, sem, *, n_tok):
    i = pl.program_id(0)

    def wait_tile():
        pltpu.make_async_copy(f_ref, xs_ref.at[pl.ds(0, TM)], sem).wait()

    def scatter(k8, carry):
        for j in range(DMA_UNROLL):
            k = k8 * DMA_UNROLL + j
            tok = i * TM + k
            pltpu.make_async_copy(f_ref.at[k], xs_ref.at[pos_ref[tok]], sem).start(priority=0)
            pltpu.make_async_copy(f_ref.at[k], xs_ref.at[pos_ref[n_tok + tok]], sem).start(priority=1)
        return carry

    lax.fori_loop(0, TM // DMA_UNROLL, scatter, 0)

    @pl.when(i == 0)
    def _():
        def fill(k8, carry):
            for j in range(DMA_UNROLL):
                pltpu.make_async_copy(f_ref.at[0], xs_ref.at[pad_ref[k8 * DMA_UNROLL + j]],
                                      sem).start(priority=j % 2)
            return carry
        lax.fori_loop(0, MOE_PAD // DMA_UNROLL, fill, 0)
        for _ in range(MOE_PAD // TM):
            wait_tile()

    wait_tile()
    wait_tile()


def _dispatch_call(pos, pad_pos, f, row0):
    n_tok = N_ROWS - row0
    off = row0 // TM
    return pl.pallas_call(
        functools.partial(_dispatch_kernel, n_tok=n_tok),
        grid_spec=pltpu.PrefetchScalarGridSpec(
            num_scalar_prefetch=2,
            grid=(n_tok // TM,),
            in_specs=[pl.BlockSpec((TM,) + TOKEN_TILE, lambda i, pos, pad: (i + off, 0, 0))],
            out_specs=pl.BlockSpec(memory_space=pl.ANY),
            scratch_shapes=[pltpu.SemaphoreType.DMA(())],
        ),
        out_shape=jax.ShapeDtypeStruct((2 * n_tok + MOE_PAD,) + TOKEN_TILE, F32),
        compiler_params=_params(("arbitrary",)),
        name="moe_dispatch",
    )(pos, pad_pos, f)


def _moe_group_kernel(te_ref, nu_ref, x_ref, wg_ref, wu_ref, wd_ref, o_ref):
    p = pl.program_id(0)
    j = pl.program_id(1)

    @pl.when(p < nu_ref[0])
    def _():
        x = x_ref[...].reshape(MOE_TM, D_MODEL).astype(BF16)
        g = _dot(x, wg_ref[0, 0])
        a = (g * jax.nn.sigmoid(g)) * _dot(x, wu_ref[0, 0])
        part = _dot(a.astype(BF16), wd_ref[0, 0]).reshape(o_ref.shape)

        @pl.when(j == 0)
        def _():
            o_ref[...] = part

        @pl.when(j > 0)
        def _():
            o_ref[...] += part

    @pl.when(p >= nu_ref[0])
    def _():
        o_ref[...] = jnp.zeros_like(o_ref)


def _moe_group_call(tile_expert, n_used, xs, wg, wu, wd, layer):
    n_tiles = xs.shape[0] // MOE_TM
    nf = D_FF // FF_CHUNK
    chunk = lambda p, j, nu: jnp.where(p < nu[0], j, nf - 1)
    return pl.pallas_call(
        _moe_group_kernel,
        grid_spec=pltpu.PrefetchScalarGridSpec(
            num_scalar_prefetch=2,
            grid=(n_tiles, nf),
            in_specs=[
                pl.BlockSpec((MOE_TM,) + TOKEN_TILE, lambda p, j, te, nu: (p, 0, 0)),
                pl.BlockSpec((1, 1, D_MODEL, FF_CHUNK),
                             lambda p, j, te, nu: (layer, te[p], 0, chunk(p, j, nu))),
                pl.BlockSpec((1, 1, D_MODEL, FF_CHUNK),
                             lambda p, j, te, nu: (layer, te[p], 0, chunk(p, j, nu))),
                pl.BlockSpec((1, 1, FF_CHUNK, D_MODEL),
                             lambda p, j, te, nu: (layer, te[p], chunk(p, j, nu), 0)),
            ],
            out_specs=pl.BlockSpec((MOE_TM,) + TOKEN_TILE, lambda p, j, te, nu: (p, 0, 0)),
        ),
        out_shape=jax.ShapeDtypeStruct(xs.shape, F32),
        compiler_params=_params(("arbitrary", "arbitrary")),
        name="moe_group",
    )(tile_expert, n_used, xs, wg, wu, wd)


def _moe_combine_kernel(to_output, n_tok, pos_ref, ys_ref, route_ref, h_ref, mod_ref, gpost_ref, o_ref,
                        ybuf, sem):
    i = pl.program_id(0)

    def request(tile):
        b = tile % 2

        def gather(k8, carry):
            for j in range(DMA_UNROLL):
                k = k8 * DMA_UNROLL + j
                tok = tile * TM + k
                pltpu.make_async_copy(ys_ref.at[pos_ref[tok]], ybuf.at[b, 0, k], sem.at[b]).start(priority=0)
                pltpu.make_async_copy(ys_ref.at[pos_ref[n_tok + tok]], ybuf.at[b, 1, k],
                                      sem.at[b]).start(priority=1)
            return carry

        lax.fori_loop(0, TM // DMA_UNROLL, gather, 0)

    @pl.when(i == 0)
    def _():
        request(i)

    @pl.when(i + 1 < pl.num_programs(0))
    def _():
        request(i + 1)

    cur = i % 2
    for slot in range(2):
        pltpu.make_async_copy(ys_ref.at[pl.ds(0, TM)], ybuf.at[cur, slot], sem.at[cur]).wait()

    route = route_ref[...]
    lane = lax.broadcasted_iota(jnp.int32, route.shape, 1)
    w1 = jnp.sum(jnp.where(lane == ROUTE_W1, route, 0.0), axis=-1, keepdims=True)
    w2 = jnp.sum(jnp.where(lane == ROUTE_W2, route, 0.0), axis=-1, keepdims=True)
    y = w1 * ybuf[cur, 0].reshape(TM, D_MODEL) + w2 * ybuf[cur, 1].reshape(TM, D_MODEL)
    hn = _tile_gate_add(h_ref[...], mod_ref[0, 5], _rms(y, gpost_ref[...]))
    o_ref[...] = _from_stream(hn) if to_output else hn


def _moe_combine_call(pos, ys, route, hs, mod, g_post, row0, to_output):
    t = N_ROWS - row0
    nt = t // TM
    off = row0 // TM
    if to_output:
        out_spec = pl.BlockSpec((BATCH, TM // BATCH, D_MODEL), lambda i, pos: (0, i, 0))
        out_shape = jax.ShapeDtypeStruct((BATCH, t // BATCH, D_MODEL), F32)
    else:
        out_spec = pl.BlockSpec((TM, D_MODEL), lambda i, pos: (i, 0))
        out_shape = jax.ShapeDtypeStruct((t, D_MODEL), F32)
    return pl.pallas_call(
        functools.partial(_moe_combine_kernel, to_output, t),
        grid_spec=pltpu.PrefetchScalarGridSpec(
            num_scalar_prefetch=1,
            grid=(nt,),
            in_specs=[
                pl.BlockSpec(memory_space=pl.ANY),
                pl.BlockSpec((TM, LANES), lambda i, pos: (i + off, 0)),
                pl.BlockSpec((TM, D_MODEL), lambda i, pos: (i + off, 0)),
                pl.BlockSpec((1, N_MOD, SUBLANES, D_MODEL), lambda i, pos: (_seg_index(i, off), 0, 0, 0)),
                pl.BlockSpec((1, D_MODEL), lambda i, pos: (0, 0)),
            ],
            out_specs=out_spec,
            scratch_shapes=[pltpu.VMEM((2, 2, TM) + TOKEN_TILE, F32), pltpu.SemaphoreType.DMA((2,))],
        ),
        out_shape=out_shape,
        compiler_params=_params(("arbitrary",)),
        name="moe_combine",
    )(pos, ys, route, hs, mod, g_post)


def _moe_call(f, hs, route, mod, wg, wu, wd, g_post, row0, layer):
    pos, pad_pos, tile_expert, n_used = _dispatch_plan(route, row0)
    xs = _dispatch_call(pos, pad_pos, f, row0)
    ys = _moe_group_call(tile_expert, n_used, xs, wg, wu, wd, layer)
    return _moe_combine_call(pos, ys, route, hs, mod, g_post, row0, to_output=row0 > 0)


def kernel(x, c, ctx, c_ctx, w_ada, b_ada, g_pre_mix, g_post_mix, g_pre_ffn, g_post_ffn, w_in,
           ssm_lam_re, ssm_lam_im, ssm_log_dt, ssm_b_re, ssm_b_im, ssm_c_re, ssm_c_im, ssm_d,
           ssm_w_glu, ssm_b_glu, conv_w, conv_b, conv_ln_g, conv_ln_b, w_out,
           ffn_w_gate, ffn_w_up, ffn_w_down, moe_w_router, moe_b_router, moe_w_gate, moe_w_up, moe_w_down):
    cin = jnp.concatenate([jnp.broadcast_to(c_ctx[None, :], (BATCH, D_MODEL)), c], axis=0)
    mod_all = _ada_call(cin, w_ada, b_ada)
    mod_all = mod_all.reshape(DEPTH, 2, BATCH, N_MOD, D_MODEL).transpose(0, 1, 3, 2, 4)

    w_in_b, w_glu_b, w_out_b = w_in.astype(BF16), ssm_w_glu.astype(BF16), w_out.astype(BF16)
    ffn_b = (ffn_w_gate.astype(BF16), ffn_w_up.astype(BF16), ffn_w_down.astype(BF16))
    moe_b = (moe_w_gate.astype(BF16), moe_w_up.astype(BF16), moe_w_down.astype(BF16))
    s5 = jax.vmap(_s5_params)(ssm_lam_re, ssm_lam_im, ssm_log_dt, ssm_b_re, ssm_b_im, ssm_c_re, ssm_c_im)

    row_vec = lambda v: v.reshape(1, -1).astype(F32)
    for l in range(DEPTH):
        mod = mod_all[l]
        if l == 0:
            u, agl, hs = _mix_in_call((ctx, x), mod, row_vec(g_pre_mix[l]), w_in_b, l)
        else:
            u, agl = _mix_in_call(hs, mod, row_vec(g_pre_mix[l]), w_in_b, l)
        yf, yb = _scan_call(u, l, *s5)
        p = dict(
            ssm_d=row_vec(ssm_d[l]), w_glu=w_glu_b, b_glu=row_vec(ssm_b_glu[l]),
            conv_w=jnp.broadcast_to(conv_w[l][:, None, :], (CONV_WIDTH, SUBLANES, D_CONV)).astype(F32),
            conv_b=row_vec(conv_b[l]), ln_g=row_vec(conv_ln_g[l]), ln_b=row_vec(conv_ln_b[l]),
            w_out=w_out_b, g_post_mix=row_vec(g_post_mix[l]), g_pre_ffn=row_vec(g_pre_ffn[l]))
        i = l // 2
        if l % 2 == 0:
            hs, f = _mix_out_call(hs, u, yf, yb, agl, mod, p, l)
            hs = _ffn_call(f, hs, mod, *ffn_b, row_vec(g_post_ffn[l]), i)
        else:
            wr = jnp.zeros((D_MODEL, LANES), F32).at[:, :N_EXPERTS].set(moe_w_router[i].astype(F32))
            wr_hi = wr.astype(BF16)
            wr_lo = (wr - wr_hi.astype(F32)).astype(BF16)
            br = jnp.zeros((1, LANES), F32).at[0, :N_EXPERTS].set(moe_b_router[i].astype(F32))
            hs, f, route = _mix_out_call(hs, u, yf, yb, agl, mod, p, l, router=(wr_hi, wr_lo, br))
            row0 = 0 if l < DEPTH - 1 else N_CTX
            hs = _moe_call(f, hs, route, mod, *moe_b, row_vec(g_post_ffn[l]), row0, i)
    return hs
```

```python
import functools
import math

import jax
import jax.numpy as jnp
from jax import lax
from jax.experimental import pallas as pl
from jax.experimental.pallas import tpu as pltpu

F32 = jnp.float32
BF16 = jnp.bfloat16

D_MODEL = 1024
BATCH = 8
SEQ = 2048
CTX_LEN = 256
DEPTH = 4
D_SSM = 512
SSM_GROUP = 16
N_SSM_GROUPS = 32
SSM_STATE = 64
D_CONV = 512
CONV_WIDTH = 31
CONV_PAD = CONV_WIDTH // 2
D_IN = D_SSM + 2 * D_CONV
D_FF = 2816
N_EXPERTS = 8
N_MOD = 6
EPS = 1e-6

N_CTX = CTX_LEN * BATCH
N_LAT = SEQ * BATCH
N_ROWS = N_CTX + N_LAT
N_STATE = N_SSM_GROUPS * SSM_STATE

SUBLANES = 8
LANES = 128
VMEM_LIMIT = 56 * 1024 * 1024

TM = 512
SCAN_T = 64
SCAN_R = SCAN_T * BATCH
SCAN_SLABS = 2
SLAB_CH = D_SSM // SCAN_SLABS
SLAB_ST = N_STATE // SCAN_SLABS
B_SLABS = 2
B_SLAB_CH = D_SSM // B_SLABS
B_SLAB_ST = N_STATE // B_SLABS
SCAN_W = 512
HALO = 128
CONV_SUB = 32
FF_CHUNK = 1408
MOE_TM = 512
MOE_PAD = N_EXPERTS * MOE_TM
DMA_UNROLL = 8
ROUTE_E1, ROUTE_E2, ROUTE_W1, ROUTE_W2 = 0, 1, 2, 3
TOKEN_TILE = (SUBLANES, D_MODEL // SUBLANES)


def _dot(a, b):
    return jnp.dot(a, b, preferred_element_type=F32)


def _rms(x, g):
    ms = jnp.mean(x * x, axis=-1, keepdims=True)
    return x * lax.rsqrt(ms + EPS) * g


def _tile_mul_add(y, scale, shift):
    rows, d = y.shape
    y3 = y.reshape(rows // SUBLANES, SUBLANES, d)
    return (y3 * (1.0 + scale)[None] + shift[None]).reshape(rows, d)


def _tile_gate_add(h, gate, r):
    rows, d = h.shape
    r3 = r.reshape(rows // SUBLANES, SUBLANES, d)
    return h + (gate[None] * r3).reshape(rows, d)


def _params(sem):
    return pltpu.CompilerParams(dimension_semantics=sem, vmem_limit_bytes=VMEM_LIMIT)


def _const_spec(shape):
    nd = len(shape)
    return pl.BlockSpec(shape, lambda *_: (0,) * nd)


def _ada_kernel(c_ref, w_ref, b_ref, o_ref):
    c = c_ref[...]
    s = c * jax.nn.sigmoid(c)
    o_ref[0] = _dot(s, w_ref[0]) + b_ref[0]


def _ada_call(cin, w_ada, b_ada):
    tn = 1536
    return pl.pallas_call(
        _ada_kernel,
        grid=(DEPTH, N_MOD * D_MODEL // tn),
        in_specs=[
            pl.BlockSpec((2 * BATCH, D_MODEL), lambda l, j: (0, 0)),
            pl.BlockSpec((1, D_MODEL, tn), lambda l, j: (l, 0, j)),
            pl.BlockSpec((1, 1, tn), lambda l, j: (l, 0, j)),
        ],
        out_specs=pl.BlockSpec((1, 2 * BATCH, tn), lambda l, j: (l, 0, j)),
        out_shape=jax.ShapeDtypeStruct((DEPTH, 2 * BATCH, N_MOD * D_MODEL), F32),
        compiler_params=_params(("arbitrary", "arbitrary")),
        name="ada_mod",
    )(cin, w_ada, b_ada.reshape(DEPTH, 1, N_MOD * D_MODEL))


def _to_stream(x):
    return jnp.swapaxes(x, 0, 1).reshape(x.shape[1] * BATCH, x.shape[2])


def _from_stream(h):
    return jnp.swapaxes(h.reshape(h.shape[0] // BATCH, BATCH, h.shape[1]), 0, 1)


def _mix_in_kernel(from_inputs, *refs):
    if from_inputs:
        ctx_ref, x_ref, mod_ref, g_ref, w_ref, u_ref, a_ref, hs_ref = refs
        is_ctx = pl.program_id(0) < N_CTX // TM

        @pl.when(is_ctx)
        def _():
            hs_ref[...] = _to_stream(ctx_ref[...])

        @pl.when(jnp.logical_not(is_ctx))
        def _():
            hs_ref[...] = _to_stream(x_ref[...])

        h = hs_ref[...]
    else:
        h_ref, mod_ref, g_ref, w_ref, u_ref, a_ref = refs
        h = h_ref[...]
    y = _rms(h, g_ref[...])
    a = _tile_mul_add(y, mod_ref[0, 1], mod_ref[0, 0]).astype(BF16)
    p = _dot(a, w_ref[0])
    u_ref[...] = p[:, :D_SSM]
    v = p[:, D_SSM:D_SSM + D_CONV]
    g = p[:, D_SSM + D_CONV:]
    a_ref[...] = v * jax.nn.sigmoid(g)


def _seg_index(i, off):
    return ((i + off) >= (N_CTX // TM)).astype(jnp.int32)


def _layer_spec(shape, layer, **kwargs):
    return pl.BlockSpec((1,) + shape, lambda *_: (layer,) + (0,) * len(shape), **kwargs)


def _mix_in_call(hs, mod, g_pre, w_in, layer):
    nt = N_ROWS // TM
    from_inputs = isinstance(hs, tuple)
    row = lambda i: (i, 0)
    if from_inputs:
        steps = TM // BATCH
        nct = N_CTX // TM
        h_specs = [pl.BlockSpec((BATCH, steps, D_MODEL), lambda i: (0, jnp.minimum(i, nct - 1), 0)),
                   pl.BlockSpec((BATCH, steps, D_MODEL), lambda i: (0, jnp.maximum(i - nct, 0), 0))]
        h_args = list(hs)
    else:
        h_specs = [pl.BlockSpec((TM, D_MODEL), row)]
        h_args = [hs]
    out_specs = [pl.BlockSpec((TM, D_SSM), row), pl.BlockSpec((TM, D_CONV), row)]
    out_shape = [jax.ShapeDtypeStruct((N_ROWS, D_SSM), F32), jax.ShapeDtypeStruct((N_ROWS, D_CONV), F32)]
    if from_inputs:
        out_specs.append(pl.BlockSpec((TM, D_MODEL), row))
        out_shape.append(jax.ShapeDtypeStruct((N_ROWS, D_MODEL), F32))
    return pl.pallas_call(
        functools.partial(_mix_in_kernel, from_inputs),
        grid=(nt,),
        in_specs=h_specs + [
            pl.BlockSpec((1, N_MOD, SUBLANES, D_MODEL), lambda i: (_seg_index(i, 0), 0, 0, 0)),
            _const_spec((1, D_MODEL)),
            _layer_spec((D_MODEL, D_IN), layer),
        ],
        out_specs=out_specs,
        out_shape=out_shape,
        compiler_params=_params(("arbitrary",)),
        name="mix_in",
    )(*h_args, mod, g_pre, w_in)


def _scan_phases(refs, cur, prv, st_re, st_im):
    uf_ref, ub_ref, bre_ref, bim_ref, cre_ref, cim_ref, lr_ref, li_ref, yf_ref, yb_ref = refs
    cur_re, cur_im = cur
    prv_re, prv_im = prv

    for d, y_ref in enumerate((yf_ref, yb_ref)):
        for s in range(SCAN_SLABS):
            st = slice(s * SLAB_ST, (s + 1) * SLAB_ST)
            y_ref[:, s * SLAB_CH:(s + 1) * SLAB_CH] = (
                _dot(cur_re[d, :, st].astype(BF16), cre_ref[0, d, s])
                + _dot(cur_im[d, :, st].astype(BF16), cim_ref[0, d, s]))

    for d, u_ref in enumerate((uf_ref, ub_ref)):
        u = u_ref[...].astype(BF16)
        for s in range(B_SLABS):
            us = u[:, s * B_SLAB_CH:(s + 1) * B_SLAB_CH]
            cur_re[d, :, s * B_SLAB_ST:(s + 1) * B_SLAB_ST] = _dot(us, bre_ref[0, d, s])
            cur_im[d, :, s * B_SLAB_ST:(s + 1) * B_SLAB_ST] = _dot(us, bim_ref[0, d, s])

    for d in range(2):
        for c in range(N_STATE // SCAN_W):
            cols = slice(c * SCAN_W, (c + 1) * SCAN_W)
            lam_r = lr_ref[0, d, :, cols]
            lam_i = li_ref[0, d, :, cols]
            sr = st_re[d, :, cols]
            si = st_im[d, :, cols]
            for t in range(SCAN_T):
                tt = t if d == 0 else SCAN_T - 1 - t
                rows = slice(tt * SUBLANES, (tt + 1) * SUBLANES)
                nr = lam_r * sr - lam_i * si + prv_re[d, rows, cols]
                ni = lam_r * si + lam_i * sr + prv_im[d, rows, cols]
                prv_re[d, rows, cols] = nr
                prv_im[d, rows, cols] = ni
                sr, si = nr, ni
            st_re[d, :, cols] = sr
            st_im[d, :, cols] = si


def _scan_kernel(*refs):
    io, (re0, im0, re1, im1, st_re, st_im) = refs[:10], refs[10:]
    k = pl.program_id(0)

    @pl.when(k == 0)
    def _():
        for buf in (re0, im0, re1, im1, st_re, st_im):
            buf[...] = jnp.zeros_like(buf)

    @pl.when(k % 2 == 0)
    def _():
        _scan_phases(io, (re0, im0), (re1, im1), st_re, st_im)

    @pl.when(k % 2 == 1)
    def _():
        _scan_phases(io, (re1, im1), (re0, im0), st_re, st_im)


def _bwd_chunk(k):
    nc = N_CTX // SCAN_R
    n = N_ROWS // SCAN_R
    return jnp.where(k < nc, nc - 1 - k, n - 1 + nc - k)


def _scan_call(u, layer, bre, bim, cre, cim, lam_r, lam_i):
    n = N_ROWS // SCAN_R
    bu_chunk = lambda k: jnp.minimum(k, n - 1)
    ch_chunk = lambda k: jnp.maximum(k - 2, 0)
    once = dict(pipeline_mode=pl.Buffered(1))
    return pl.pallas_call(
        _scan_kernel,
        grid=(n + 2,),
        in_specs=[
            pl.BlockSpec((SCAN_R, D_SSM), lambda k: (bu_chunk(k), 0)),
            pl.BlockSpec((SCAN_R, D_SSM), lambda k: (_bwd_chunk(bu_chunk(k)), 0)),
            _layer_spec((2, B_SLABS, B_SLAB_CH, B_SLAB_ST), layer, **once),
            _layer_spec((2, B_SLABS, B_SLAB_CH, B_SLAB_ST), layer, **once),
            _layer_spec((2, SCAN_SLABS, SLAB_ST, SLAB_CH), layer, **once),
            _layer_spec((2, SCAN_SLABS, SLAB_ST, SLAB_CH), layer, **once),
            _layer_spec((2, SUBLANES, N_STATE), layer, **once),
            _layer_spec((2, SUBLANES, N_STATE), layer, **once),
        ],
        out_specs=[
            pl.BlockSpec((SCAN_R, D_SSM), lambda k: (ch_chunk(k), 0)),
            pl.BlockSpec((SCAN_R, D_SSM), lambda k: (_bwd_chunk(ch_chunk(k)), 0)),
        ],
        out_shape=[
            jax.ShapeDtypeStruct((N_ROWS, D_SSM), F32),
            jax.ShapeDtypeStruct((N_ROWS, D_SSM), F32),
        ],
        scratch_shapes=[pltpu.VMEM((2, SCAN_R, N_STATE), F32)] * 4
        + [pltpu.VMEM((2, SUBLANES, N_STATE), F32)] * 2,
        compiler_params=_params(("arbitrary",)),
        name="s5_scan",
    )(u, u, bre, bim, cre, cim, lam_r, lam_i)


def _s5_params(lam_re, lam_im, log_dt, b_re, b_im, c_re, c_im):
    lr, li = lam_re.astype(F32), lam_im.astype(F32)
    dt = jnp.exp(log_dt.astype(F32))[..., None]
    mag = jnp.exp(lr * dt)
    lbr, lbi = mag * jnp.cos(li * dt), mag * jnp.sin(li * dt)
    den = lr * lr + li * li
    qr = ((lbr - 1.0) * lr + lbi * li) / den
    qi = (lbi * lr - (lbr - 1.0) * li) / den
    br, bi = b_re.astype(F32), b_im.astype(F32)
    bbr = qr[..., None] * br - qi[..., None] * bi
    bbi = qr[..., None] * bi + qi[..., None] * br
    def block_diag(m, slabs):
        gs = N_SSM_GROUPS // slabs
        a, b = m.shape[2:]
        rows = jnp.swapaxes(m.reshape(2, slabs, gs, a, b), 3, 4).reshape(2, slabs, gs * b, a)
        col = jnp.arange(gs * a)
        spread = (jnp.arange(a)[:, None] == col[None, :] % a).astype(F32)
        own = (jnp.arange(gs * b)[:, None] // b) == (col[None, :] // a)
        tiled = jnp.einsum('dsra,ac->dsrc', rows, spread, precision=lax.Precision.HIGHEST)
        return jnp.where(own, tiled, 0.0).astype(BF16)

    def b_blocks(m):
        return block_diag(m, B_SLABS)

    def c_blocks(m):
        return block_diag(m, SCAN_SLABS)

    def lanes(v):
        return jnp.broadcast_to(v.reshape(2, 1, N_STATE), (2, SUBLANES, N_STATE))

    return (b_blocks(bbr), b_blocks(bbi), c_blocks(c_re.astype(F32)), c_blocks(-c_im.astype(F32)),
            lanes(lbr), lanes(lbi))


def _mix_out_kernel(with_router, h_ref, u_ref, yf_ref, yb_ref, a_ref, ap_ref, an_ref, mod_ref,
                    dsk_ref, wglu_ref, bglu_ref, cw_ref, cb_ref, lng_ref, lnb_ref, wout_ref,
                    gpost_ref, gffn_ref, *rest):
    if with_router:
        wrh_ref, wrl_ref, br_ref, hn_ref, f_ref, route_ref, pad, cvs = rest
    else:
        hn_ref, f_ref, pad, cvs = rest
    i = pl.program_id(0)
    nct = N_CTX // TM
    nt = N_ROWS // TM
    first = jnp.logical_or(i == 0, i == nct)
    last = jnp.logical_or(i == nct - 1, i == nt - 1)
    edge = CONV_PAD * BATCH

    y = yf_ref[...] + yb_ref[...] + dsk_ref[...] * u_ref[...]
    z = jax.nn.gelu(y)
    s = z * jax.nn.sigmoid(_dot(z.astype(BF16), wglu_ref[0]) + bglu_ref[...])

    pad[0:edge, :] = jnp.where(first, 0.0, ap_ref[HALO - edge:HALO, :])
    pad[edge:edge + TM, :] = a_ref[...]
    pad[edge + TM:2 * edge + TM, :] = jnp.where(last, 0.0, an_ref[0:edge, :])

    def conv_block(j, carry):
        r0 = pl.multiple_of(j * CONV_SUB, CONV_SUB)
        ng = CONV_SUB // BATCH
        step = lambda s: pad[pl.ds(r0 + s * BATCH, BATCH), :]
        win = [step(s) for s in range(ng)]
        acc = [jnp.zeros((BATCH, D_CONV), F32) for _ in range(ng)]
        for k in range(CONV_WIDTH):
            w = cw_ref[k]
            acc = [acc[q] + win[q] * w for q in range(ng)]
            if k + 1 < CONV_WIDTH:
                win = win[1:] + [step(k + ng)]
        for q in range(ng):
            cvs[pl.ds(r0 + q * BATCH, BATCH), :] = acc[q]
        return carry

    lax.fori_loop(0, TM // CONV_SUB, conv_block, 0)
    cv = cvs[...] + cb_ref[...]
    mu = jnp.mean(cv, axis=-1, keepdims=True)
    xc = cv - mu
    var = jnp.mean(xc * xc, axis=-1, keepdims=True)
    cv = xc * lax.rsqrt(var + EPS) * lng_ref[...] + lnb_ref[...]
    cv = cv * jax.nn.sigmoid(cv)

    out = _dot(s.astype(BF16), wout_ref[0, 0:D_SSM, :]) + _dot(cv.astype(BF16), wout_ref[0, D_SSM:, :])
    hn = _tile_gate_add(h_ref[...], mod_ref[0, 2], _rms(out, gpost_ref[...]))
    hn_ref[...] = hn
    f = _tile_mul_add(_rms(hn, gffn_ref[...]), mod_ref[0, 4], mod_ref[0, 3])
    if with_router:
        f_ref[...] = f.reshape(f_ref.shape)
    else:
        f_ref[...] = f.astype(BF16)

    if with_router:
        f_hi = f.astype(BF16)
        f_lo = (f - f_hi.astype(F32)).astype(BF16)
        logits = (_dot(f_hi, wrh_ref[...]) + _dot(f_lo, wrh_ref[...]) + _dot(f_hi, wrl_ref[...])
                  + br_ref[...])
        lane = lax.broadcasted_iota(jnp.int32, logits.shape, 1)
        neg = jnp.float32(-jnp.inf)
        logits = jnp.where(lane < N_EXPERTS, logits, neg)
        m1 = jnp.max(logits, axis=-1, keepdims=True)
        i1 = jnp.min(jnp.where(logits == m1, lane, LANES), axis=-1, keepdims=True)
        rest_l = jnp.where(lane == i1, neg, logits)
        m2 = jnp.max(rest_l, axis=-1, keepdims=True)
        i2 = jnp.min(jnp.where(rest_l == m2, lane, LANES), axis=-1, keepdims=True)
        e2 = jnp.exp(m2 - m1)
        den = 1.0 + e2
        route = jnp.where(lane == ROUTE_E1, i1.astype(F32), 0.0)
        route = jnp.where(lane == ROUTE_E2, i2.astype(F32), route)
        route = jnp.where(lane == ROUTE_W1, 1.0 / den, route)
        route_ref[...] = jnp.where(lane == ROUTE_W2, e2 / den, route)


def _mix_out_call(hs, u, yf, yb, agl, mod, p, layer, router=None):
    nt = N_ROWS // TM
    hb = TM // HALO
    nh = N_ROWS // HALO
    row = lambda i: (i, 0)
    in_specs = [
        pl.BlockSpec((TM, D_MODEL), row),
        pl.BlockSpec((TM, D_SSM), row),
        pl.BlockSpec((TM, D_SSM), row),
        pl.BlockSpec((TM, D_SSM), row),
        pl.BlockSpec((TM, D_CONV), row),
        pl.BlockSpec((HALO, D_CONV), lambda i: (jnp.maximum(i * hb - 1, 0), 0)),
        pl.BlockSpec((HALO, D_CONV), lambda i: (jnp.minimum((i + 1) * hb, nh - 1), 0)),
        pl.BlockSpec((1, N_MOD, SUBLANES, D_MODEL), lambda i: (_seg_index(i, 0), 0, 0, 0)),
        _const_spec((1, D_SSM)),
        _layer_spec((D_SSM, D_SSM), layer),
        _const_spec((1, D_SSM)),
        _const_spec((CONV_WIDTH, SUBLANES, D_CONV)),
        _const_spec((1, D_CONV)),
        _const_spec((1, D_CONV)),
        _const_spec((1, D_CONV)),
        _layer_spec((D_MODEL, D_MODEL), layer),
        _const_spec((1, D_MODEL)),
        _const_spec((1, D_MODEL)),
    ]
    args = [hs, u, yf, yb, agl, agl, agl, mod, p["ssm_d"], p["w_glu"], p["b_glu"], p["conv_w"],
            p["conv_b"], p["ln_g"], p["ln_b"], p["w_out"], p["g_post_mix"], p["g_pre_ffn"]]
    out_specs = [pl.BlockSpec((TM, D_MODEL), row)]
    out_shape = [jax.ShapeDtypeStruct((N_ROWS, D_MODEL), F32)]
    if router is None:
        out_specs.append(pl.BlockSpec((TM, D_MODEL), row))
        out_shape.append(jax.ShapeDtypeStruct((N_ROWS, D_MODEL), BF16))
    else:
        in_specs += [_const_spec((D_MODEL, LANES)), _const_spec((D_MODEL, LANES)), _const_spec((1, LANES))]
        args += list(router)
        out_specs += [pl.BlockSpec((TM,) + TOKEN_TILE, lambda i: (i, 0, 0)), pl.BlockSpec((TM, LANES), row)]
        out_shape += [jax.ShapeDtypeStruct((N_ROWS,) + TOKEN_TILE, F32),
                      jax.ShapeDtypeStruct((N_ROWS, LANES), F32)]
    return pl.pallas_call(
        functools.partial(_mix_out_kernel, router is not None),
        grid=(nt,),
        in_specs=in_specs,
        out_specs=out_specs,
        out_shape=out_shape,
        scratch_shapes=[
            pltpu.VMEM((TM + 2 * CONV_PAD * BATCH, D_CONV), F32),
            pltpu.VMEM((TM, D_CONV), F32),
        ],
        compiler_params=_params(("arbitrary",)),
        name="mix_out",
    )(*args)


def _ffn_kernel(f_ref, h_ref, mod_ref, wg_ref, wu_ref, wd_ref, gpost_ref, o_ref):
    f = f_ref[...]
    acc = None
    for j in range(D_FF // FF_CHUNK):
        cols = slice(j * FF_CHUNK, (j + 1) * FF_CHUNK)
        g = _dot(f, wg_ref[0, :, cols])
        a = (g * jax.nn.sigmoid(g)) * _dot(f, wu_ref[0, :, cols])
        part = _dot(a.astype(BF16), wd_ref[0, cols, :])
        acc = part if acc is None else acc + part
    o_ref[...] = _tile_gate_add(h_ref[...], mod_ref[0, 5], _rms(acc, gpost_ref[...]))


def _ffn_call(f, hs, mod, wg, wu, wd, g_post, layer):
    nt = N_ROWS // TM
    row = lambda i: (i, 0)
    once = dict(pipeline_mode=pl.Buffered(1))
    return pl.pallas_call(
        _ffn_kernel,
        grid=(nt,),
        in_specs=[
            pl.BlockSpec((TM, D_MODEL), row),
            pl.BlockSpec((TM, D_MODEL), row),
            pl.BlockSpec((1, N_MOD, SUBLANES, D_MODEL), lambda i: (_seg_index(i, 0), 0, 0, 0)),
            _layer_spec((D_MODEL, D_FF), layer, **once),
            _layer_spec((D_MODEL, D_FF), layer, **once),
            _layer_spec((D_FF, D_MODEL), layer, **once),
            _const_spec((1, D_MODEL)),
        ],
        out_specs=pl.BlockSpec((TM, D_MODEL), row),
        out_shape=jax.ShapeDtypeStruct((N_ROWS, D_MODEL), F32),
        compiler_params=_params(("arbitrary",)),
        name="ffn_dense",
    )(f, hs, mod, wg, wu, wd, g_post)


def _dispatch_plan(route, row0):
    r = route[row0:]
    e = jnp.concatenate([r[:, ROUTE_E1], r[:, ROUTE_E2]]).astype(jnp.int32)
    ids = jnp.arange(N_EXPERTS, dtype=jnp.int32)
    onehot = (e[:, None] == ids[None, :]).astype(jnp.int32)
    csum = jnp.cumsum(onehot, axis=0)
    counts = csum[-1]
    rank = jnp.sum((csum - onehot) * onehot, axis=1)
    padded = ((counts + MOE_TM - 1) // MOE_TM) * MOE_TM
    ends = jnp.cumsum(padded)
    offs = ends - padded
    pos = jnp.sum(onehot * offs[None, :], axis=1) + rank

    padlen = padded - counts
    cp_end = jnp.cumsum(padlen)
    cp = cp_end - padlen
    j = jnp.arange(MOE_PAD, dtype=jnp.int32)
    ej = jnp.minimum(jnp.sum((j[:, None] >= cp_end[None, :]).astype(jnp.int32), axis=1), N_EXPERTS - 1)
    pad_pos = jnp.where(j < cp_end[-1], offs[ej] + counts[ej] + (j - cp[ej]), ends[-1] + (j - cp_end[-1]))

    n_tiles = (2 * r.shape[0] + MOE_PAD) // MOE_TM
    starts = jnp.arange(n_tiles, dtype=jnp.int32) * MOE_TM
    tile_expert = jnp.minimum(jnp.sum((starts[:, None] >= ends[None, :]).astype(jnp.int32), axis=1),
                              N_EXPERTS - 1)
    n_used = (ends[-1] // MOE_TM).reshape(1)
    return pos, pad_pos.astype(jnp.int32), tile_expert, n_used


def _dispatch_kernel(pos_ref, pad_ref, f_ref, xs_ref, sem, *, n_tok):
    i = pl.program_id(0)

    def wait_tile():
        pltpu.make_async_copy(f_ref, xs_ref.at[pl.ds(0, TM)], sem).wait()

    def scatter(k8, carry):
        for j in range(DMA_UNROLL):
            k = k8 * DMA_UNROLL + j
            tok = i * TM + k
            pltpu.make_async_copy(f_ref.at[k], xs_ref.at[pos_ref[tok]], sem).start(priority=0)
            pltpu.make_async_copy(f_ref.at[k], xs_ref.at[pos_ref[n_tok + tok]], sem).start(priority=1)
        return carry

    lax.fori_loop(0, TM // DMA_UNROLL, scatter, 0)

    @pl.when(i == 0)
    def _():
        def fill(k8, carry):
            for j in range(DMA_UNROLL):
                pltpu.make_async_copy(f_ref.at[0], xs_ref.at[pad_ref[k8 * DMA_UNROLL + j]],
                                      sem).start(priority=j % 2)
            return carry
        lax.fori_loop(0, MOE_PAD // DMA_UNROLL, fill, 0)
        for _ in range(MOE_PAD // TM):
            wait_tile()

    wait_tile()
    wait_tile()


def _dispatch_call(pos, pad_pos, f, row0):
    n_tok = N_ROWS - row0
    off = row0 // TM
    return pl.pallas_call(
        functools.partial(_dispatch_kernel, n_tok=n_tok),
        grid_spec=pltpu.PrefetchScalarGridSpec(
            num_scalar_prefetch=2,
            grid=(n_tok // TM,),
            in_specs=[pl.BlockSpec((TM,) + TOKEN_TILE, lambda i, pos, pad: (i + off, 0, 0))],
            out_specs=pl.BlockSpec(memory_space=pl.ANY),
            scratch_shapes=[pltpu.SemaphoreType.DMA(())],
        ),
        out_shape=jax.ShapeDtypeStruct((2 * n_tok + MOE_PAD,) + TOKEN_TILE, F32),
        compiler_params=_params(("arbitrary",)),
        name="moe_dispatch",
    )(pos, pad_pos, f)


def _moe_group_kernel(te_ref, nu_ref, x_ref, wg_ref, wu_ref, wd_ref, o_ref):
    p = pl.program_id(0)

    @pl.when(p < nu_ref[0])
    def _():
        x = x_ref[...].reshape(MOE_TM, D_MODEL).astype(BF16)
        acc = None
        for j in range(D_FF // FF_CHUNK):
            cols = slice(j * FF_CHUNK, (j + 1) * FF_CHUNK)
            g = _dot(x, wg_ref[0, 0, :, cols])
            a = (g * jax.nn.sigmoid(g)) * _dot(x, wu_ref[0, 0, :, cols])
            part = _dot(a.astype(BF16), wd_ref[0, 0, cols, :])
            acc = part if acc is None else acc + part
        o_ref[...] = acc.reshape(o_ref.shape)

    @pl.when(p >= nu_ref[0])
    def _():
        o_ref[...] = jnp.zeros_like(o_ref)


def _moe_group_call(tile_expert, n_used, xs, wg, wu, wd, layer):
    n_tiles = xs.shape[0] // MOE_TM
    return pl.pallas_call(
        _moe_group_kernel,
        grid_spec=pltpu.PrefetchScalarGridSpec(
            num_scalar_prefetch=2,
            grid=(n_tiles,),
            in_specs=[
                pl.BlockSpec((MOE_TM,) + TOKEN_TILE, lambda p, te, nu: (p, 0, 0)),
                pl.BlockSpec((1, 1, D_MODEL, D_FF), lambda p, te, nu: (layer, te[p], 0, 0)),
                pl.BlockSpec((1, 1, D_MODEL, D_FF), lambda p, te, nu: (layer, te[p], 0, 0)),
                pl.BlockSpec((1, 1, D_FF, D_MODEL), lambda p, te, nu: (layer, te[p], 0, 0)),
            ],
            out_specs=pl.BlockSpec((MOE_TM,) + TOKEN_TILE, lambda p, te, nu: (p, 0, 0)),
        ),
        out_shape=jax.ShapeDtypeStruct(xs.shape, F32),
        compiler_params=_params(("arbitrary",)),
        name="moe_group",
    )(tile_expert, n_used, xs, wg, wu, wd)


def _moe_combine_kernel(to_output, n_tok, pos_ref, ys_ref, route_ref, h_ref, mod_ref, gpost_ref, o_ref,
                        ybuf, sem):
    i = pl.program_id(0)

    def request(tile):
        b = tile % 2

        def gather(k8, carry):
            for j in range(DMA_UNROLL):
                k = k8 * DMA_UNROLL + j
                tok = tile * TM + k
                pltpu.make_async_copy(ys_ref.at[pos_ref[tok]], ybuf.at[b, 0, k], sem.at[b]).start(priority=0)
                pltpu.make_async_copy(ys_ref.at[pos_ref[n_tok + tok]], ybuf.at[b, 1, k],
                                      sem.at[b]).start(priority=1)
            return carry

        lax.fori_loop(0, TM // DMA_UNROLL, gather, 0)

    @pl.when(i == 0)
    def _():
        request(i)

    @pl.when(i + 1 < pl.num_programs(0))
    def _():
        request(i + 1)

    cur = i % 2
    for slot in range(2):
        pltpu.make_async_copy(ys_ref.at[pl.ds(0, TM)], ybuf.at[cur, slot], sem.at[cur]).wait()

    route = route_ref[...]
    lane = lax.broadcasted_iota(jnp.int32, route.shape, 1)
    w1 = jnp.sum(jnp.where(lane == ROUTE_W1, route, 0.0), axis=-1, keepdims=True)
    w2 = jnp.sum(jnp.where(lane == ROUTE_W2, route, 0.0), axis=-1, keepdims=True)
    y = w1 * ybuf[cur, 0].reshape(TM, D_MODEL) + w2 * ybuf[cur, 1].reshape(TM, D_MODEL)
    hn = _tile_gate_add(h_ref[...], mod_ref[0, 5], _rms(y, gpost_ref[...]))
    o_ref[...] = _from_stream(hn) if to_output else hn


def _moe_combine_call(pos, ys, route, hs, mod, g_post, row0, to_output):
    t = N_ROWS - row0
    nt = t // TM
    off = row0 // TM
    if to_output:
        out_spec = pl.BlockSpec((BATCH, TM // BATCH, D_MODEL), lambda i, pos: (0, i, 0))
        out_shape = jax.ShapeDtypeStruct((BATCH, t // BATCH, D_MODEL), F32)
    else:
        out_spec = pl.BlockSpec((TM, D_MODEL), lambda i, pos: (i, 0))
        out_shape = jax.ShapeDtypeStruct((t, D_MODEL), F32)
    return pl.pallas_call(
        functools.partial(_moe_combine_kernel, to_output, t),
        grid_spec=pltpu.PrefetchScalarGridSpec(
            num_scalar_prefetch=1,
            grid=(nt,),
            in_specs=[
                pl.BlockSpec(memory_space=pl.ANY),
                pl.BlockSpec((TM, LANES), lambda i, pos: (i + off, 0)),
                pl.BlockSpec((TM, D_MODEL), lambda i, pos: (i + off, 0)),
                pl.BlockSpec((1, N_MOD, SUBLANES, D_MODEL), lambda i, pos: (_seg_index(i, off), 0, 0, 0)),
                pl.BlockSpec((1, D_MODEL), lambda i, pos: (0, 0)),
            ],
            out_specs=out_spec,
            scratch_shapes=[pltpu.VMEM((2, 2, TM) + TOKEN_TILE, F32), pltpu.SemaphoreType.DMA((2,))],
        ),
        out_shape=out_shape,
        compiler_params=_params(("arbitrary",)),
        name="moe_combine",
    )(pos, ys, route, hs, mod, g_post)


def _moe_call(f, hs, route, mod, wg, wu, wd, g_post, row0, layer):
    pos, pad_pos, tile_expert, n_used = _dispatch_plan(route, row0)
    xs = _dispatch_call(pos, pad_pos, f, row0)
    ys = _moe_group_call(tile_expert, n_used, xs, wg, wu, wd, layer)
    return _moe_combine_call(pos, ys, route, hs, mod, g_post, row0, to_output=row0 > 0)


def kernel(x, c, ctx, c_ctx, w_ada, b_ada, g_pre_mix, g_post_mix, g_pre_ffn, g_post_ffn, w_in,
           ssm_lam_re, ssm_lam_im, ssm_log_dt, ssm_b_re, ssm_b_im, ssm_c_re, ssm_c_im, ssm_d,
           ssm_w_glu, ssm_b_glu, conv_w, conv_b, conv_ln_g, conv_ln_b, w_out,
           ffn_w_gate, ffn_w_up, ffn_w_down, moe_w_router, moe_b_router, moe_w_gate, moe_w_up, moe_w_down):
    cin = jnp.concatenate([jnp.broadcast_to(c_ctx[None, :], (BATCH, D_MODEL)), c], axis=0)
    mod_all = _ada_call(cin, w_ada, b_ada)
    mod_all = mod_all.reshape(DEPTH, 2, BATCH, N_MOD, D_MODEL).transpose(0, 1, 3, 2, 4)

    w_in_b, w_glu_b, w_out_b = w_in.astype(BF16), ssm_w_glu.astype(BF16), w_out.astype(BF16)
    ffn_b = (ffn_w_gate.astype(BF16), ffn_w_up.astype(BF16), ffn_w_down.astype(BF16))
    moe_b = (moe_w_gate.astype(BF16), moe_w_up.astype(BF16), moe_w_down.astype(BF16))
    s5 = jax.vmap(_s5_params)(ssm_lam_re, ssm_lam_im, ssm_log_dt, ssm_b_re, ssm_b_im, ssm_c_re, ssm_c_im)

    row_vec = lambda v: v.reshape(1, -1).astype(F32)
    for l in range(DEPTH):
        mod = mod_all[l]
        if l == 0:
            u, agl, hs = _mix_in_call((ctx, x), mod, row_vec(g_pre_mix[l]), w_in_b, l)
        else:
            u, agl = _mix_in_call(hs, mod, row_vec(g_pre_mix[l]), w_in_b, l)
        yf, yb = _scan_call(u, l, *s5)
        p = dict(
            ssm_d=row_vec(ssm_d[l]), w_glu=w_glu_b, b_glu=row_vec(ssm_b_glu[l]),
            conv_w=jnp.broadcast_to(conv_w[l][:, None, :], (CONV_WIDTH, SUBLANES, D_CONV)).astype(F32),
            conv_b=row_vec(conv_b[l]), ln_g=row_vec(conv_ln_g[l]), ln_b=row_vec(conv_ln_b[l]),
            w_out=w_out_b, g_post_mix=row_vec(g_post_mix[l]), g_pre_ffn=row_vec(g_pre_ffn[l]))
        i = l // 2
        if l % 2 == 0:
            hs, f = _mix_out_call(hs, u, yf, yb, agl, mod, p, l)
            hs = _ffn_call(f, hs, mod, *ffn_b, row_vec(g_post_ffn[l]), i)
        else:
            wr = jnp.zeros((D_MODEL, LANES), F32).at[:, :N_EXPERTS].set(moe_w_router[i].astype(F32))
            wr_hi = wr.astype(BF16)
            wr_lo = (wr - wr_hi.astype(F32)).astype(BF16)
            br = jnp.zeros((1, LANES), F32).at[0, :N_EXPERTS].set(moe_b_router[i].astype(F32))
            hs, f, route = _mix_out_call(hs, u, yf, yb, agl, mod, p, l, router=(wr_hi, wr_lo, br))
            row0 = 0 if l < DEPTH - 1 else N_CTX
            hs = _moe_call(f, hs, route, mod, *moe_b, row_vec(g_post_ffn[l]), row0, i)
    return hs
```
